```python
import jax, jax.numpy as jnp
from jax import lax
import numpy as np

D_MODEL = 1024
BATCH = 8
SEQ = 2048
DEPTH = 4

N_META = 16
D_FF = 2816
RES_HALF = 0.5
EPS = 1e-6
SB_HEADS = 16
SB_HEAD_DIM = D_MODEL // SB_HEADS
SB_BLOCK = 128
GLA_HEADS = 4
GLA_DK = D_MODEL // 2
GLA_DV = D_MODEL
GLA_HK = GLA_DK // GLA_HEADS
GLA_HV = GLA_DV // GLA_HEADS
GLA_GATE_RANK = 16
GLA_TAU = 16.0
GLA_CHUNK = 64
GLA_IN = 2 * GLA_DK + 2 * GLA_DV + GLA_GATE_RANK
N_SB = (DEPTH + 1) // 2
N_GLA = DEPTH // 2

kernel_name = "hybrid_stickbreak_gla_macaron"


def _rmsnorm(x, g):
    xf = x.astype(jnp.float32)
    y = xf * lax.rsqrt(jnp.mean(xf * xf, axis=-1, keepdims=True) + EPS)
    return (y * g.astype(jnp.float32)).astype(x.dtype)


def _swiglu(h, w_gu, w_down):
    g, u = jnp.split(h @ w_gu, 2, axis=-1)
    return (jax.nn.silu(g) * u) @ w_down


def _stick_breaking(h, w_qkv, g_q, g_k, w_o):
    B, L, _ = h.shape
    qkv = (h @ w_qkv).reshape(B, L, 3, SB_HEADS, SB_HEAD_DIM)
    q = _rmsnorm(qkv[:, :, 0], g_q)
    k = _rmsnorm(qkv[:, :, 1], g_k)
    v = qkv[:, :, 2]
    pad = (-L) % SB_BLOCK
    def prep(t):
        return jnp.pad(t, ((0, 0), (pad, 0), (0, 0), (0, 0))).transpose(0, 2, 1, 3)
    q, k, v = prep(q), prep(k), prep(v)
    Lp = L + pad
    scale = SB_HEAD_DIM ** -0.5
    outs = []
    for t0 in range(0, Lp, SB_BLOCK):
        t1 = t0 + SB_BLOCK
        z = jnp.einsum('bhtd,bhsd->bhts', q[:, :, t0:t1], k[:, :, :t1]).astype(jnp.float32) * scale
        t_pos = jnp.arange(t0, t1)[:, None]
        s_pos = jnp.arange(t1)[None, :]
        mask = (s_pos < t_pos) & (s_pos >= pad)
        log_1m = jnp.where(mask, jax.nn.log_sigmoid(-z), 0.0)
        log_after = lax.cumsum(log_1m, axis=3, reverse=True) - log_1m
        w = jnp.where(mask, jnp.exp(jax.nn.log_sigmoid(z) + log_after), 0.0)
        outs.append(jnp.einsum('bhts,bhsd->bhtd', w.astype(v.dtype), v[:, :, :t1]))
    o = jnp.concatenate(outs, axis=2)[:, :, pad:]
    o = o.transpose(0, 2, 1, 3).reshape(B, L, SB_HEADS * SB_HEAD_DIM)
    return o @ w_o


def _gla(h, w_in, w_gate_up, b_gate, g_out, w_o):
    B, L, _ = h.shape
    f32 = jnp.float32
    proj = h @ w_in
    q, k, v, r, g_low = jnp.split(
        proj, [GLA_DK, 2 * GLA_DK, 2 * GLA_DK + GLA_DV, 2 * GLA_DK + 2 * GLA_DV], axis=-1)
    log_a = jax.nn.log_sigmoid((g_low @ w_gate_up + b_gate).astype(f32)) / GLA_TAU
    pad = (-L) % GLA_CHUNK
    Lp = L + pad
    N = Lp // GLA_CHUNK
    def chunk(t, hd):
        t = jnp.pad(t, ((0, 0), (pad, 0), (0, 0)))
        return t.reshape(B, N, GLA_CHUNK, GLA_HEADS, hd).transpose(0, 3, 1, 2, 4)
    qc = chunk(q.astype(f32) * GLA_HK ** -0.5, GLA_HK)
    kc = chunk(k.astype(f32), GLA_HK)
    vc = chunk(v.astype(f32), GLA_HV)
    ac = chunk(log_a, GLA_HK)
    b = jnp.cumsum(ac, axis=3)
    b_last = b[:, :, :, -1:, :]
    q_dec = qc * jnp.exp(b)
    k_dec = kc * jnp.exp(-b)
    k_st = kc * jnp.exp(b_last - b)
    causal = jnp.tril(jnp.ones((GLA_CHUNK, GLA_CHUNK), dtype=bool))
    att = jnp.where(causal, jnp.einsum('bhnck,bhnsk->bhncs', q_dec, k_dec), 0.0)
    o_intra = jnp.einsum('bhncs,bhnsv->bhncv', att, vc)

    def step(S, xs):
        q_n, k_n, v_n, dec_n = xs
        o_n = jnp.einsum('bhck,bhkv->bhcv', q_n, S)
        S = S * jnp.swapaxes(dec_n, -1, -2) + jnp.einsum('bhck,bhcv->bhkv', k_n, v_n)
        return S, o_n

    xs = tuple(jnp.moveaxis(t, 2, 0) for t in (q_dec, k_st, vc, jnp.exp(b_last)))
    S0 = jnp.zeros((B, GLA_HEADS, GLA_HK, GLA_HV), f32)
    _, o_inter = lax.scan(step, S0, xs)
    o = o_intra + jnp.moveaxis(o_inter, 0, 2)
    o = o.transpose(0, 2, 3, 1, 4).reshape(B, Lp, GLA_HEADS, GLA_HV)[:, pad:]
    o = o * lax.rsqrt(jnp.mean(o * o, axis=-1, keepdims=True) + EPS)
    o = o * g_out.astype(f32).reshape(GLA_HEADS, GLA_HV)
    o = o.reshape(B, L, GLA_DV) * jax.nn.silu(r.astype(f32))
    return o.astype(h.dtype) @ w_o


def setup_inputs(seed: int = 0) -> dict:
    key = jax.random.key(seed)
    ks = jax.random.split(key, 20)
    f32 = jnp.float32
    def dense(k, shape, fan_in):
        return jax.random.normal(k, shape, f32) * fan_in ** -0.5
    def gain(k, shape):
        return 1.0 + 0.02 * jax.random.normal(k, shape, f32)
    return {
        "x": jax.random.normal(ks[0], (BATCH, SEQ, D_MODEL), f32),
        "meta": jax.random.normal(ks[1], (N_META, D_MODEL), f32),
        "ffn_a_norm": gain(ks[2], (DEPTH, D_MODEL)),
        "ffn_a_w_gu": dense(ks[3], (DEPTH, D_MODEL, 2 * D_FF), D_MODEL),
        "ffn_a_w_down": dense(ks[4], (DEPTH, D_FF, D_MODEL), D_FF),
        "mix_norm": gain(ks[5], (DEPTH, D_MODEL)),
        "sb_w_qkv": dense(ks[6], (N_SB, D_MODEL, 3 * D_MODEL), D_MODEL),
        "sb_q_norm": gain(ks[7], (N_SB, SB_HEAD_DIM)),
        "sb_k_norm": gain(ks[8], (N_SB, SB_HEAD_DIM)),
        "sb_w_o": dense(ks[9], (N_SB, D_MODEL, D_MODEL), D_MODEL),
        "gla_w_in": dense(ks[10], (N_GLA, D_MODEL, GLA_IN), D_MODEL),
        "gla_w_gate_up": dense(ks[11], (N_GLA, GLA_GATE_RANK, GLA_DK), GLA_GATE_RANK),
        "gla_b_gate": 0.1 * jax.random.normal(ks[12], (N_GLA, GLA_DK), f32),
        "gla_out_norm": gain(ks[13], (N_GLA, GLA_DV)),
        "gla_w_o": dense(ks[14], (N_GLA, GLA_DV, D_MODEL), GLA_DV),
        "ffn_b_norm": gain(ks[15], (DEPTH, D_MODEL)),
        "ffn_b_w_gu": dense(ks[16], (DEPTH, D_MODEL, 2 * D_FF), D_MODEL),
        "ffn_b_w_down": dense(ks[17], (DEPTH, D_FF, D_MODEL), D_FF),
    }


def reference(x, meta, ffn_a_norm, ffn_a_w_gu, ffn_a_w_down, mix_norm,
              sb_w_qkv, sb_q_norm, sb_k_norm, sb_w_o,
              gla_w_in, gla_w_gate_up, gla_b_gate, gla_out_norm, gla_w_o,
              ffn_b_norm, ffn_b_w_gu, ffn_b_w_down):
    B = x.shape[0]
    m = jnp.broadcast_to(meta.astype(x.dtype)[None], (B, N_META, x.shape[-1]))
    h = jnp.concatenate([m, x], axis=1)
    for i in range(DEPTH):
        h = h + RES_HALF * _swiglu(_rmsnorm(h, ffn_a_norm[i]), ffn_a_w_gu[i], ffn_a_w_down[i])
        hn = _rmsnorm(h, mix_norm[i])
        j = i // 2
        if i % 2 == 0:
            h = h + _stick_breaking(hn, sb_w_qkv[j], sb_q_norm[j], sb_k_norm[j], sb_w_o[j])
        else:
            h = h + _gla(hn, gla_w_in[j], gla_w_gate_up[j], gla_b_gate[j], gla_out_norm[j], gla_w_o[j])
        h = h + RES_HALF * _swiglu(_rmsnorm(h, ffn_b_norm[i]), ffn_b_w_gu[i], ffn_b_w_down[i])
    return h[:, N_META:]
```

```python
import functools

import jax
import jax.numpy as jnp
from jax import lax
from jax.experimental import pallas as pl
from jax.experimental.pallas import tpu as pltpu

F32 = jnp.float32
BF16 = jnp.bfloat16

N_META = 16
RES_HALF = 0.5
EPS = 1e-6
SB_HEADS = 16
SB_HEAD_DIM = 64
GLA_HEADS = 4
GLA_GATE_RANK = 16
GLA_TAU = 16.0
GLA_CHUNK = 64

LANES = 128
SB_TILE = 128
SB_SKIP_LOG = 104.0
VMEM_LIMIT = 56 * 1024 * 1024

_NT = (((1,), (1,)), ((), ()))


def _row_tile(t):
    best = None
    for cand in range(16, min(t, 1024) + 1, 16):
        if t % cand == 0:
            best = cand
    assert best is not None, t
    return best


def _split_hi_lo(a):
    hi = a.astype(BF16)
    lo = (a - hi.astype(F32)).astype(BF16)
    return hi, lo


def _rmsnorm_rows(x, g):
    ms = jnp.mean(x * x, axis=-1, keepdims=True)
    return x * lax.rsqrt(ms + EPS) * g


def _softplus(z):
    return jnp.maximum(z, 0.0) + jnp.log(1.0 + jnp.exp(-jnp.abs(z)))


def _ffn_body(h_ref, g_ref, wg_ref, wu_ref, wd_ref, o_ref, xn_ref, acc_ref):
    j = pl.program_id(1)

    @pl.when(j == 0)
    def _():
        xn_ref[...] = _rmsnorm_rows(h_ref[...], g_ref[...]).astype(BF16)
        acc_ref[...] = jnp.zeros_like(acc_ref)

    xn = xn_ref[...]
    g = jnp.dot(xn, wg_ref[...], preferred_element_type=F32)
    u = jnp.dot(xn, wu_ref[...], preferred_element_type=F32)
    a = (g * jax.nn.sigmoid(g) * u).astype(BF16)
    acc_ref[...] += jnp.dot(a, wd_ref[...], preferred_element_type=F32)

    @pl.when(j == pl.num_programs(1) - 1)
    def _():
        o_ref[...] = h_ref[...] + RES_HALF * acc_ref[...]


def _ffn(h, gain, w_gu, w_down, *, tf=256):
    t, d = h.shape
    dff = w_down.shape[0]
    tm = _row_tile(t)
    nf = dff // tf
    assert dff % tf == 0
    return pl.pallas_call(
        _ffn_body,
        out_shape=jax.ShapeDtypeStruct((t, d), F32),
        grid=(t // tm, nf),
        in_specs=[
            pl.BlockSpec((tm, d), lambda i, j: (i, 0)),
            pl.BlockSpec((1, d), lambda i, j: (0, 0)),
            pl.BlockSpec((d, tf), lambda i, j: (0, j)),
            pl.BlockSpec((d, tf), lambda i, j: (0, j + nf)),
            pl.BlockSpec((tf, d), lambda i, j: (j, 0)),
        ],
        out_specs=pl.BlockSpec((tm, d), lambda i, j: (i, 0)),
        scratch_shapes=[pltpu.VMEM((tm, d), BF16), pltpu.VMEM((tm, d), F32)],
        compiler_params=pltpu.CompilerParams(
            dimension_semantics=("parallel", "arbitrary"), vmem_limit_bytes=VMEM_LIMIT),
        name="ffn",
    )(h, gain.reshape(1, d), w_gu, w_gu, w_down)


def _out_proj_body(h_ref, o_ref, w_ref, out_ref):
    out_ref[...] = h_ref[...] + jnp.dot(o_ref[...], w_ref[...], preferred_element_type=F32)


def _out_proj(h, o, w):
    t, d = h.shape
    k = o.shape[1]
    tm = _row_tile(t)
    return pl.pallas_call(
        _out_proj_body,
        out_shape=jax.ShapeDtypeStruct((t, d), F32),
        grid=(t // tm,),
        in_specs=[
            pl.BlockSpec((tm, d), lambda i: (i, 0)),
            pl.BlockSpec((tm, k), lambda i: (i, 0)),
            pl.BlockSpec((k, d), lambda i: (0, 0)),
        ],
        out_specs=pl.BlockSpec((tm, d), lambda i: (i, 0)),
        compiler_params=pltpu.CompilerParams(
            dimension_semantics=("parallel",), vmem_limit_bytes=VMEM_LIMIT),
        name="out_proj",
    )(h, o, w)


def _sb_proj_body(h_ref, g_ref, w_ref, hg_ref, seg_ref, o_ref, xn_ref):
    j = pl.program_id(1)

    @pl.when(j == 0)
    def _():
        xn_ref[...] = _rmsnorm_rows(h_ref[...], g_ref[...]).astype(BF16)

    y = jnp.dot(xn_ref[...], w_ref[...], preferred_element_type=F32)

    @pl.when(j < 2)
    def _():
        seg = seg_ref[...]
        for c in range(y.shape[1] // LANES):
            yc = y[:, c * LANES:(c + 1) * LANES]
            hi, lo = _split_hi_lo(yc * yc)
            ms = jnp.dot(jnp.concatenate([hi, lo], axis=1), seg, preferred_element_type=F32)
            gain = hg_ref[0, :, c * LANES:(c + 1) * LANES]
            o_ref[0, :, c * LANES:(c + 1) * LANES] = (yc * lax.rsqrt(ms + EPS) * gain).astype(BF16)

    @pl.when(j == 2)
    def _():
        o_ref[0] = y.astype(BF16)


def _sb_proj(h, gain, w_qkv, head_gains, seg):
    t, d = h.shape
    n = w_qkv.shape[1] // 3
    tm = _row_tile(t)
    return pl.pallas_call(
        _sb_proj_body,
        out_shape=jax.ShapeDtypeStruct((3, t, n), BF16),
        grid=(t // tm, 3),
        in_specs=[
            pl.BlockSpec((tm, d), lambda i, j: (i, 0)),
            pl.BlockSpec((1, d), lambda i, j: (0, 0)),
            pl.BlockSpec((d, n), lambda i, j: (0, j)),
            pl.BlockSpec((1, 1, n), lambda i, j: (jnp.minimum(j, 1), 0, 0)),
            pl.BlockSpec((2 * LANES, LANES), lambda i, j: (0, 0)),
        ],
        out_specs=pl.BlockSpec((1, tm, n), lambda i, j: (j, i, 0)),
        scratch_shapes=[pltpu.VMEM((tm, d), BF16)],
        compiler_params=pltpu.CompilerParams(
            dimension_semantics=("parallel", "arbitrary"), vmem_limit_bytes=VMEM_LIMIT),
        name="sb_proj",
    )(h, gain.reshape(1, d), w_qkv, head_gains, seg)


def _sb_tile(qh, kblk, vblk, cum, carry, acc, mask):
    z = lax.dot_general(qh, kblk, _NT, preferred_element_type=F32)
    sp = _softplus(z)
    spm = sp if mask is None else jnp.where(mask, sp, 0.0)
    hi, lo = _split_hi_lo(spm)
    cr = jnp.dot(jnp.concatenate([hi, lo], axis=1), cum, preferred_element_type=F32)
    later = carry + cr[:, :SB_TILE]
    w = jnp.exp(z - sp - later)
    if mask is not None:
        w = jnp.where(mask, w, 0.0)
    acc = acc + jnp.dot(w.astype(BF16), vblk, preferred_element_type=F32)
    return carry + cr[:, SB_TILE:], acc


def _sb_attn_body(q_ref, k_ref, v_ref, cum_ref, o_ref, carry_ref, acc_ref, *, seq):
    n_full = seq // SB_TILE
    rem = seq - n_full * SB_TILE
    lane = lax.broadcasted_iota(jnp.int32, (1, LANES), 1)
    head_lanes = [lane < SB_HEAD_DIM, lane >= SB_HEAD_DIM]

    def window(t0, rows, first_k0, first_mask, n_prev):
        q = q_ref[0, pl.ds(t0, rows), :]
        qh = [jnp.where(m, q, jnp.zeros_like(q)) for m in head_lanes]
        cum = cum_ref[...]
        kblk = k_ref[0, pl.ds(first_k0, SB_TILE), :]
        vblk = v_ref[0, pl.ds(first_k0, SB_TILE), :]
        zero = jnp.zeros((rows, SB_TILE), F32)
        low = None
        for hd in range(2):
            carry, acc = _sb_tile(qh[hd], kblk, vblk, cum, zero, zero, first_mask)
            carry_ref[hd, :rows, :] = carry
            acc_ref[hd, :rows, :] = acc
            low = carry if low is None else jnp.minimum(low, carry)

        def cond(state):
            j, lowest = state
            return jnp.logical_and(j >= 0, lowest < SB_SKIP_LOG)

        def body(state):
            j, _ = state
            s0 = pl.multiple_of(j * SB_TILE, SB_TILE)
            kb = k_ref[0, pl.ds(s0, SB_TILE), :]
            vb = v_ref[0, pl.ds(s0, SB_TILE), :]
            cm = cum_ref[...]
            lo_c = None
            for hd in range(2):
                c, a = _sb_tile(qh[hd], kb, vb, cm, carry_ref[hd, :rows, :], acc_ref[hd, :rows, :], None)
                carry_ref[hd, :rows, :] = c
                acc_ref[hd, :rows, :] = a
                lo_c = c if lo_c is None else jnp.minimum(lo_c, c)
            return j - 1, jnp.min(lo_c)

        lax.while_loop(cond, body, (jnp.asarray(n_prev - 1, jnp.int32), jnp.min(low)))
        out = jnp.where(head_lanes[0], acc_ref[0, :rows, :], acc_ref[1, :rows, :])
        o_ref[0, pl.ds(t0, rows), :] = out.astype(o_ref.dtype)

    row = lax.broadcasted_iota(jnp.int32, (SB_TILE, SB_TILE), 0)
    col = lax.broadcasted_iota(jnp.int32, (SB_TILE, SB_TILE), 1)

    def full_window(i, c):
        t0 = pl.multiple_of(i * SB_TILE, SB_TILE)
        window(t0, SB_TILE, t0, col < row, i)
        return c

    lax.fori_loop(0, n_full, full_window, 0)

    if rem:
        rrow = lax.broadcasted_iota(jnp.int32, (rem, SB_TILE), 0)
        rcol = lax.broadcasted_iota(jnp.int32, (rem, SB_TILE), 1)
        shift = SB_TILE - rem
        mask = jnp.logical_and(rcol >= shift, rcol < rrow + shift)
        window(n_full * SB_TILE, rem, seq - SB_TILE, mask, n_full)


def _sb_attn(qkv, cum, batch, seq):
    n = qkv.shape[2]
    assert seq >= SB_TILE and seq % 16 == 0 and n % LANES == 0
    qkv = qkv.reshape(3, batch, seq, n)
    body = functools.partial(_sb_attn_body, seq=seq)

    def spec(which):
        return pl.BlockSpec((None, 1, seq, LANES), lambda b, p, which=which: (which, b, 0, p))

    out = pl.pallas_call(
        body,
        out_shape=jax.ShapeDtypeStruct((batch, seq, n), BF16),
        grid=(batch, n // LANES),
        in_specs=[spec(0), spec(1), spec(2),
                  pl.BlockSpec((2 * SB_TILE, 2 * SB_TILE), lambda b, p: (0, 0))],
        out_specs=pl.BlockSpec((1, seq, LANES), lambda b, p: (b, 0, p)),
        scratch_shapes=[pltpu.VMEM((2, SB_TILE, SB_TILE), F32), pltpu.VMEM((2, SB_TILE, SB_TILE), F32)],
        compiler_params=pltpu.CompilerParams(
            dimension_semantics=("parallel", "parallel"), vmem_limit_bytes=VMEM_LIMIT),
        name="sb_attn",
    )(qkv, qkv, qkv, cum)
    return out.reshape(batch * seq, n)


def _sb_constants():
    r = jnp.arange(2 * LANES)[:, None] % LANES
    c = jnp.arange(LANES)[None, :]
    seg = jnp.where(r // SB_HEAD_DIM == c // SB_HEAD_DIM, 1.0 / SB_HEAD_DIM, 0.0).astype(BF16)
    j = jnp.arange(2 * SB_TILE)[:, None] % SB_TILE
    s = jnp.arange(2 * SB_TILE)[None, :]
    cum = jnp.where(jnp.logical_or(s >= SB_TILE, j > s), 1.0, 0.0).astype(BF16)
    return seg, cum


def _sb_layer(h, gain, w_qkv, g_q, g_k, w_o, batch, seq):
    seg, cum = _sb_constants()
    scale = SB_HEAD_DIM ** -0.5
    head_gains = jnp.stack([jnp.tile(g_q, SB_HEADS) * scale, jnp.tile(g_k, SB_HEADS)])[:, None, :]
    qkv = _sb_proj(h, gain, w_qkv, head_gains.astype(F32), seg)
    o = _sb_attn(qkv, cum, batch, seq)
    return _out_proj(h, o, w_o)


def _gla_proj_body(h_ref, g_ref, w_ref, wlow_ref, wup_ref, b_ref, o_ref, la_ref, xn_ref):
    j = pl.program_id(1)

    @pl.when(j == 0)
    def _():
        xn = _rmsnorm_rows(h_ref[...], g_ref[...]).astype(BF16)
        xn_ref[...] = xn
        g_low = jnp.dot(xn, wlow_ref[...], preferred_element_type=F32).astype(BF16)
        pre = jnp.dot(g_low, wup_ref[...], preferred_element_type=F32) + b_ref[...]
        la_ref[...] = (jnp.minimum(pre, 0.0) - jnp.log(1.0 + jnp.exp(-jnp.abs(pre)))) / GLA_TAU

    o_ref[...] = jnp.dot(xn_ref[...], w_ref[...], preferred_element_type=F32)


def _gla_proj(h, gain, w_main, w_low, w_up, b_gate, *, tn=512):
    t, d = h.shape
    n = w_main.shape[1]
    dk = w_up.shape[1]
    tm = _row_tile(t)
    assert n % tn == 0
    return pl.pallas_call(
        _gla_proj_body,
        out_shape=(jax.ShapeDtypeStruct((t, n), F32), jax.ShapeDtypeStruct((t, dk), F32)),
        grid=(t // tm, n // tn),
        in_specs=[
            pl.BlockSpec((tm, d), lambda i, j: (i, 0)),
            pl.BlockSpec((1, d), lambda i, j: (0, 0)),
            pl.BlockSpec((d, tn), lambda i, j: (0, j)),
            pl.BlockSpec((d, LANES), lambda i, j: (0, 0)),
            pl.BlockSpec((LANES, dk), lambda i, j: (0, 0)),
            pl.BlockSpec((1, dk), lambda i, j: (0, 0)),
        ],
        out_specs=(pl.BlockSpec((tm, tn), lambda i, j: (i, j)),
                   pl.BlockSpec((tm, dk), lambda i, j: (i, 0))),
        scratch_shapes=[pltpu.VMEM((tm, d), BF16)],
        compiler_params=pltpu.CompilerParams(
            dimension_semantics=("parallel", "arbitrary"), vmem_limit_bytes=VMEM_LIMIT),
        name="gla_proj",
    )(h, gain.reshape(1, d), w_main, w_low, w_up, b_gate.reshape(1, dk))


def _gla_core_body(q_ref, k_ref, v_ref, r_ref, la_ref, gout_ref, tril_ref, o_ref, st_ref, *, seq, hk):
    c = GLA_CHUNK
    lead = seq % c
    n_chunks = seq // c
    scale = hk ** -0.5
    st_ref[...] = jnp.zeros_like(st_ref)
    row = lax.broadcasted_iota(jnp.int32, (c, c), 0)
    col = lax.broadcasted_iota(jnp.int32, (c, c), 1)
    causal = col <= row

    def chunk(q, k, v, r, a):
        hi, lo = _split_hi_lo(a)
        b = jnp.dot(tril_ref[...], jnp.concatenate([hi, lo], axis=0), preferred_element_type=F32)
        b_last = b[c - 1:c, :]
        qd = (q * scale * jnp.exp(b)).astype(BF16)
        kd = (k * jnp.exp(-b)).astype(BF16)
        ks = (k * jnp.exp(b_last - b)).astype(BF16)
        att = lax.dot_general(qd, kd, _NT, preferred_element_type=F32)
        att = jnp.where(causal, att, 0.0).astype(BF16)
        st = st_ref[...]
        o = jnp.dot(att, v.astype(BF16), preferred_element_type=F32)
        o = o + lax.dot_general(qd, st.astype(BF16), _NT, preferred_element_type=F32)
        kv = jnp.dot(v.T.astype(BF16), ks, preferred_element_type=F32)
        st_ref[...] = st * jnp.exp(b_last) + kv
        o = o * lax.rsqrt(jnp.mean(o * o, axis=-1, keepdims=True) + EPS) * gout_ref[...]
        return (o * (r * jax.nn.sigmoid(r))).astype(o_ref.dtype)

    if lead:
        def padded(ref):
            x = ref[0, 0:lead, :]
            return jnp.concatenate([jnp.zeros((c - lead, x.shape[1]), x.dtype), x], axis=0)
        o0 = chunk(padded(q_ref), padded(k_ref), padded(v_ref), padded(r_ref), padded(la_ref))
        o_ref[0, 0:lead, :] = o0[c - lead:, :]

    def step(i, carry):
        t0 = pl.multiple_of(lead + i * c, 16)
        rows = pl.ds(t0, c)
        o_ref[0, rows, :] = chunk(q_ref[0, rows, :], k_ref[0, rows, :], v_ref[0, rows, :],
                                  r_ref[0, rows, :], la_ref[0, rows, :])
        return carry

    lax.fori_loop(0, n_chunks, step, 0)


def _gla_core(proj, la, g_out, batch, seq, dk, dv):
    hk, hv = dk // GLA_HEADS, dv // GLA_HEADS
    assert hk % LANES == 0 and hv % LANES == 0 and seq % 16 == 0 and (seq % GLA_CHUNK) % 16 == 0
    proj = proj.reshape(batch, seq, proj.shape[1])
    la = la.reshape(batch, seq, dk)
    tril = (jnp.arange(GLA_CHUNK)[:, None] >= jnp.arange(2 * GLA_CHUNK)[None, :] % GLA_CHUNK).astype(BF16)
    body = functools.partial(_gla_core_body, seq=seq, hk=hk)
    nk, nv = dk // hk, dv // hv
    out = pl.pallas_call(
        body,
        out_shape=jax.ShapeDtypeStruct((batch, seq, dv), BF16),
        grid=(batch, GLA_HEADS),
        in_specs=[
            pl.BlockSpec((1, seq, hk), lambda b, h: (b, 0, h)),
            pl.BlockSpec((1, seq, hk), lambda b, h: (b, 0, nk + h)),
            pl.BlockSpec((1, seq, hv), lambda b, h: (b, 0, (2 * dk) // hv + h)),
            pl.BlockSpec((1, seq, hv), lambda b, h: (b, 0, (2 * dk) // hv + nv + h)),
            pl.BlockSpec((1, seq, hk), lambda b, h: (b, 0, h)),
            pl.BlockSpec((1, hv), lambda b, h: (0, h)),
            pl.BlockSpec((GLA_CHUNK, 2 * GLA_CHUNK), lambda b, h: (0, 0)),
        ],
        out_specs=pl.BlockSpec((1, seq, hv), lambda b, h: (b, 0, h)),
        scratch_shapes=[pltpu.VMEM((hv, hk), F32)],
        compiler_params=pltpu.CompilerParams(
            dimension_semantics=("parallel", "parallel"), vmem_limit_bytes=VMEM_LIMIT),
        name="gla_core",
    )(proj, proj, proj, proj, la, g_out.reshape(1, dv), tril)
    return out.reshape(batch * seq, dv)


def _gla_layer(h, gain, w_in, w_gate_up, b_gate, g_out, w_o, batch, seq):
    d = h.shape[1]
    dk = w_gate_up.shape[1]
    dv = g_out.shape[0]
    n_main = 2 * dk + 2 * dv
    w_main = w_in[:, :n_main]
    w_low = jnp.pad(w_in[:, n_main:], ((0, 0), (0, LANES - GLA_GATE_RANK)))
    w_up = jnp.pad(w_gate_up, ((0, LANES - GLA_GATE_RANK), (0, 0)))
    proj, la = _gla_proj(h, gain, w_main, w_low, w_up, b_gate)
    o = _gla_core(proj, la, g_out, batch, seq, dk, dv)
    return _out_proj(h, o, w_o)


def kernel(x, meta, ffn_a_norm, ffn_a_w_gu, ffn_a_w_down, mix_norm, sb_w_qkv, sb_q_norm, sb_k_norm, sb_w_o, gla_w_in, gla_w_gate_up, gla_b_gate, gla_out_norm, gla_w_o, ffn_b_norm, ffn_b_w_gu, ffn_b_w_down):
    batch, _, d = x.shape
    depth = ffn_a_norm.shape[0]
    m = jnp.broadcast_to(meta.astype(x.dtype)[None], (batch, meta.shape[0], d))
    h = jnp.concatenate([m, x], axis=1)
    seq = h.shape[1]
    h = h.reshape(batch * seq, d)
    bf = lambda w: w.astype(BF16)
    a_gu, a_down, b_gu, b_down = bf(ffn_a_w_gu), bf(ffn_a_w_down), bf(ffn_b_w_gu), bf(ffn_b_w_down)
    w_qkv, w_sbo, w_in, w_up, w_glao = bf(sb_w_qkv), bf(sb_w_o), bf(gla_w_in), bf(gla_w_gate_up), bf(gla_w_o)
    for i in range(depth):
        h = _ffn(h, ffn_a_norm[i], a_gu[i], a_down[i])
        j = i // 2
        if i % 2 == 0:
            h = _sb_layer(h, mix_norm[i], w_qkv[j], sb_q_norm[j], sb_k_norm[j], w_sbo[j], batch, seq)
        else:
            h = _gla_layer(h, mix_norm[i], w_in[j], w_up[j], gla_b_gate[j], gla_out_norm[j], w_glao[j], batch, seq)
        h = _ffn(h, ffn_b_norm[i], b_gu[i], b_down[i])
    return h.reshape(batch, seq, d)[:, meta.shape[0]:]
```

```python
import functools

import jax
import jax.numpy as jnp
from jax import lax
from jax.experimental import pallas as pl
from jax.experimental.pallas import tpu as pltpu

F32 = jnp.float32
BF16 = jnp.bfloat16

N_META = 16
RES_HALF = 0.5
EPS = 1e-6
SB_HEADS = 16
SB_HEAD_DIM = 64
GLA_HEADS = 4
GLA_GATE_RANK = 16
GLA_TAU = 16.0
GLA_CHUNK = 64

LANES = 128
SB_TILE = 128
SB_SKIP_LOG = 104.0
VMEM_LIMIT = 56 * 1024 * 1024

_NT = (((1,), (1,)), ((), ()))


def _row_tile(t):
    best = None
    for cand in range(16, min(t, 1024) + 1, 16):
        if t % cand == 0:
            best = cand
    assert best is not None, t
    return best


def _split_hi_lo(a):
    hi = a.astype(BF16)
    lo = (a - hi.astype(F32)).astype(BF16)
    return hi, lo


def _rmsnorm_rows(x, g):
    ms = jnp.mean(x * x, axis=-1, keepdims=True)
    return x * lax.rsqrt(ms + EPS) * g


def _softplus(z):
    return jnp.maximum(z, 0.0) + jnp.log(1.0 + jnp.exp(-jnp.abs(z)))


def _ffn_body(h_ref, g_ref, wg_ref, wu_ref, wd_ref, o_ref, xn_ref, acc_ref):
    j = pl.program_id(1)

    @pl.when(j == 0)
    def _():
        xn_ref[...] = _rmsnorm_rows(h_ref[...], g_ref[...]).astype(BF16)
        acc_ref[...] = jnp.zeros_like(acc_ref)

    xn = xn_ref[...]
    g = jnp.dot(xn, wg_ref[...], preferred_element_type=F32)
    u = jnp.dot(xn, wu_ref[...], preferred_element_type=F32)
    a = (g * jax.nn.sigmoid(g) * u).astype(BF16)
    acc_ref[...] += jnp.dot(a, wd_ref[...], preferred_element_type=F32)

    @pl.when(j == pl.num_programs(1) - 1)
    def _():
        o_ref[...] = h_ref[...] + RES_HALF * acc_ref[...]


def _ffn(h, gain, w_gu, w_down, *, tf=256):
    t, d = h.shape
    dff = w_down.shape[0]
    tm = _row_tile(t)
    nf = dff // tf
    assert dff % tf == 0
    return pl.pallas_call(
        _ffn_body,
        out_shape=jax.ShapeDtypeStruct((t, d), F32),
        grid=(t // tm, nf),
        in_specs=[
            pl.BlockSpec((tm, d), lambda i, j: (i, 0)),
            pl.BlockSpec((1, d), lambda i, j: (0, 0)),
            pl.BlockSpec((d, tf), lambda i, j: (0, j)),
            pl.BlockSpec((d, tf), lambda i, j: (0, j + nf)),
            pl.BlockSpec((tf, d), lambda i, j: (j, 0)),
        ],
        out_specs=pl.BlockSpec((tm, d), lambda i, j: (i, 0)),
        scratch_shapes=[pltpu.VMEM((tm, d), BF16), pltpu.VMEM((tm, d), F32)],
        compiler_params=pltpu.CompilerParams(
            dimension_semantics=("parallel", "arbitrary"), vmem_limit_bytes=VMEM_LIMIT),
        name="ffn",
    )(h, gain.reshape(1, d), w_gu, w_gu, w_down)


def _out_proj_body(h_ref, o_ref, w_ref, out_ref):
    out_ref[...] = h_ref[...] + jnp.dot(o_ref[...], w_ref[...], preferred_element_type=F32)


def _out_proj(h, o, w):
    t, d = h.shape
    k = o.shape[1]
    tm = _row_tile(t)
    return pl.pallas_call(
        _out_proj_body,
        out_shape=jax.ShapeDtypeStruct((t, d), F32),
        grid=(t // tm,),
        in_specs=[
            pl.BlockSpec((tm, d), lambda i: (i, 0)),
            pl.BlockSpec((tm, k), lambda i: (i, 0)),
            pl.BlockSpec((k, d), lambda i: (0, 0)),
        ],
        out_specs=pl.BlockSpec((tm, d), lambda i: (i, 0)),
        compiler_params=pltpu.CompilerParams(
            dimension_semantics=("parallel",), vmem_limit_bytes=VMEM_LIMIT),
        name="out_proj",
    )(h, o, w)


def _sb_proj_body(h_ref, g_ref, w_ref, hg_ref, seg_ref, o_ref, xn_ref):
    j = pl.program_id(1)

    @pl.when(j == 0)
    def _():
        xn_ref[...] = _rmsnorm_rows(h_ref[...], g_ref[...]).astype(BF16)

    y = jnp.dot(xn_ref[...], w_ref[...], preferred_element_type=F32)

    @pl.when(j < 2)
    def _():
        seg = seg_ref[...]
        for c in range(y.shape[1] // LANES):
            yc = y[:, c * LANES:(c + 1) * LANES]
            hi, lo = _split_hi_lo(yc * yc)
            ms = jnp.dot(jnp.concatenate([hi, lo], axis=1), seg, preferred_element_type=F32)
            gain = hg_ref[0, :, c * LANES:(c + 1) * LANES]
            o_ref[0, :, c * LANES:(c + 1) * LANES] = (yc * lax.rsqrt(ms + EPS) * gain).astype(BF16)

    @pl.when(j == 2)
    def _():
        o_ref[0] = y.astype(BF16)


def _sb_proj(h, gain, w_qkv, head_gains, seg):
    t, d = h.shape
    n = w_qkv.shape[1] // 3
    tm = _row_tile(t)
    return pl.pallas_call(
        _sb_proj_body,
        out_shape=jax.ShapeDtypeStruct((3, t, n), BF16),
        grid=(t // tm, 3),
        in_specs=[
            pl.BlockSpec((tm, d), lambda i, j: (i, 0)),
            pl.BlockSpec((1, d), lambda i, j: (0, 0)),
            pl.BlockSpec((d, n), lambda i, j: (0, j)),
            pl.BlockSpec((1, 1, n), lambda i, j: (jnp.minimum(j, 1), 0, 0)),
            pl.BlockSpec((2 * LANES, LANES), lambda i, j: (0, 0)),
        ],
        out_specs=pl.BlockSpec((1, tm, n), lambda i, j: (j, i, 0)),
        scratch_shapes=[pltpu.VMEM((tm, d), BF16)],
        compiler_params=pltpu.CompilerParams(
            dimension_semantics=("parallel", "arbitrary"), vmem_limit_bytes=VMEM_LIMIT),
        name="sb_proj",
    )(h, gain.reshape(1, d), w_qkv, head_gains, seg)


def _sb_attn_body(q_ref, k_ref, v_ref, cum_ref, o_ref, qp_ref, kp_ref, vp_ref, carry_ref, acc_ref,
                  flag_ref, *, seq, group):
    tile = SB_TILE
    nw = -(-seq // tile)
    lp = nw * tile
    ng = -(-nw // group)
    big = jnp.float32(3.0e38)

    for src, dst in ((q_ref, qp_ref), (k_ref, kp_ref), (v_ref, vp_ref)):
        dst[0:seq, :] = src[0]
        if lp > seq:
            dst[seq:lp, :] = jnp.zeros((lp - seq, LANES), dst.dtype)
    carry_ref[nw] = jnp.zeros(carry_ref.shape[1:], F32)
    acc_ref[nw] = jnp.zeros(acc_ref.shape[1:], F32)

    lane = lax.broadcasted_iota(jnp.int32, (1, LANES), 1)
    head0 = lane < SB_HEAD_DIM
    row = lax.broadcasted_iota(jnp.int32, (2 * tile, tile), 0)
    col = lax.broadcasted_iota(jnp.int32, (2 * tile, tile), 1)
    causal = col < jnp.where(row >= tile, row - tile, row)

    def group_step(g, d, first):
        zs, sps, hls, vbs, slots, nexts = [], [], [], [], [], []
        for u in range(group):
            i = g * group + u
            i_c = jnp.minimum(i, nw - 1)
            valid = jnp.logical_and(i < nw, i >= d)
            slots.append(jnp.where(valid, i, nw))
            nexts.append(jnp.logical_and(i < nw, i >= d + 1))
            q = qp_ref[pl.ds(pl.multiple_of(i_c * tile, tile), tile), :]
            zero = jnp.zeros_like(q)
            q2 = jnp.concatenate([jnp.where(head0, q, zero), jnp.where(head0, zero, q)], axis=0)
            kt = pl.multiple_of(jnp.maximum(i_c - d, 0) * tile, tile)
            z = lax.dot_general(q2, kp_ref[pl.ds(kt, tile), :], _NT, preferred_element_type=F32)
            sp = _softplus(z)
            hi, lo = _split_hi_lo(jnp.where(causal, sp, 0.0) if first else sp)
            zs.append(z)
            sps.append(sp)
            hls.append(jnp.concatenate([hi, lo], axis=1))
            vbs.append(vp_ref[pl.ds(kt, tile), :])
        cr = jnp.dot(jnp.concatenate(hls, axis=0), cum_ref[...], preferred_element_type=F32)
        low = big
        for u in range(group):
            cr_u = cr[u * 2 * tile:(u + 1) * 2 * tile]
            later, total = cr_u[:, :tile], cr_u[:, tile:]
            if not first:
                carry_old = carry_ref[slots[u]]
                later, total = later + carry_old, total + carry_old
            w = jnp.exp(zs[u] - sps[u] - later)
            if first:
                w = jnp.where(causal, w, 0.0)
            acc = jnp.dot(w.astype(BF16), vbs[u], preferred_element_type=F32)
            if not first:
                acc = acc + acc_ref[slots[u]]
            carry_ref[slots[u]] = total
            acc_ref[slots[u]] = acc
            low = jnp.minimum(low, jnp.where(nexts[u], jnp.min(total), big))
        flag_ref[g] = (low < SB_SKIP_LOG).astype(jnp.int32)

    def first_group(g, count):
        group_step(g, 0, True)
        return count + flag_ref[g]

    def later_step(state):
        d, _ = state

        def one_group(g, count):
            @pl.when(flag_ref[g] != 0)
            def _():
                group_step(g, d, False)
            return count + flag_ref[g]

        return d + 1, lax.fori_loop(0, ng, one_group, jnp.int32(0))

    pending = lax.fori_loop(0, ng, first_group, jnp.int32(0))
    lax.while_loop(lambda state: state[1] > 0, later_step, (jnp.int32(1), pending))

    def emit(i, start, rows):
        a = acc_ref[i]
        out = jnp.where(head0, a[:tile], a[tile:])
        o_ref[0, pl.ds(start, rows), :] = out[:rows].astype(o_ref.dtype)

    def emit_full(i, c):
        emit(i, pl.multiple_of(i * tile, tile), tile)
        return c

    lax.fori_loop(0, seq // tile, emit_full, 0)
    if lp > seq:
        emit(nw - 1, (nw - 1) * tile, seq - (nw - 1) * tile)


def _sb_attn(qkv, cum, batch, seq, *, group=6):
    n = qkv.shape[2]
    assert seq % 16 == 0 and n % LANES == 0
    qkv = qkv.reshape(3, batch, seq, n)
    nw = -(-seq // SB_TILE)
    ng = -(-nw // group)
    body = functools.partial(_sb_attn_body, seq=seq, group=group)

    def spec(which):
        return pl.BlockSpec((None, 1, seq, LANES), lambda b, p, which=which: (which, b, 0, p))

    padded = pltpu.VMEM((nw * SB_TILE, LANES), BF16)
    state = pltpu.VMEM((nw + 1, 2 * SB_TILE, SB_TILE), F32)
    out = pl.pallas_call(
        body,
        out_shape=jax.ShapeDtypeStruct((batch, seq, n), BF16),
        grid=(batch, n // LANES),
        in_specs=[spec(0), spec(1), spec(2),
                  pl.BlockSpec((2 * SB_TILE, 2 * SB_TILE), lambda b, p: (0, 0))],
        out_specs=pl.BlockSpec((1, seq, LANES), lambda b, p: (b, 0, p)),
        scratch_shapes=[padded, padded, padded, state, state, pltpu.SMEM((ng,), jnp.int32)],
        compiler_params=pltpu.CompilerParams(
            dimension_semantics=("parallel", "parallel"), vmem_limit_bytes=VMEM_LIMIT),
        name="sb_attn",
    )(qkv, qkv, qkv, cum)
    return out.reshape(batch * seq, n)


def _sb_constants():
    r = jnp.arange(2 * LANES)[:, None] % LANES
    c = jnp.arange(LANES)[None, :]
    seg = jnp.where(r // SB_HEAD_DIM == c // SB_HEAD_DIM, 1.0 / SB_HEAD_DIM, 0.0).astype(BF16)
    j = jnp.arange(2 * SB_TILE)[:, None] % SB_TILE
    s = jnp.arange(2 * SB_TILE)[None, :]
    cum = jnp.where(jnp.logical_or(s >= SB_TILE, j > s), 1.0, 0.0).astype(BF16)
    return seg, cum


def _sb_layer(h, gain, w_qkv, g_q, g_k, w_o, batch, seq):
    seg, cum = _sb_constants()
    scale = SB_HEAD_DIM ** -0.5
    head_gains = jnp.stack([jnp.tile(g_q, SB_HEADS) * scale, jnp.tile(g_k, SB_HEADS)])[:, None, :]
    qkv = _sb_proj(h, gain, w_qkv, head_gains.astype(F32), seg)
    o = _sb_attn(qkv, cum, batch, seq)
    return _out_proj(h, o, w_o)


def _gla_proj_body(h_ref, g_ref, w_ref, wlow_ref, wup_ref, b_ref, o_ref, la_ref, xn_ref):
    j = pl.program_id(1)

    @pl.when(j == 0)
    def _():
        xn = _rmsnorm_rows(h_ref[...], g_ref[...]).astype(BF16)
        xn_ref[...] = xn
        g_low = jnp.dot(xn, wlow_ref[...], preferred_element_type=F32).astype(BF16)
        pre = jnp.dot(g_low, wup_ref[...], preferred_element_type=F32) + b_ref[...]
        la_ref[...] = (jnp.minimum(pre, 0.0) - jnp.log(1.0 + jnp.exp(-jnp.abs(pre)))) / GLA_TAU

    o_ref[...] = jnp.dot(xn_ref[...], w_ref[...], preferred_element_type=F32)


def _gla_proj(h, gain, w_main, w_low, w_up, b_gate, *, tn=512):
    t, d = h.shape
    n = w_main.shape[1]
    dk = w_up.shape[1]
    tm = _row_tile(t)
    assert n % tn == 0
    return pl.pallas_call(
        _gla_proj_body,
        out_shape=(jax.ShapeDtypeStruct((t, n), F32), jax.ShapeDtypeStruct((t, dk), F32)),
        grid=(t // tm, n // tn),
        in_specs=[
            pl.BlockSpec((tm, d), lambda i, j: (i, 0)),
            pl.BlockSpec((1, d), lambda i, j: (0, 0)),
            pl.BlockSpec((d, tn), lambda i, j: (0, j)),
            pl.BlockSpec((d, LANES), lambda i, j: (0, 0)),
            pl.BlockSpec((LANES, dk), lambda i, j: (0, 0)),
            pl.BlockSpec((1, dk), lambda i, j: (0, 0)),
        ],
        out_specs=(pl.BlockSpec((tm, tn), lambda i, j: (i, j)),
                   pl.BlockSpec((tm, dk), lambda i, j: (i, 0))),
        scratch_shapes=[pltpu.VMEM((tm, d), BF16)],
        compiler_params=pltpu.CompilerParams(
            dimension_semantics=("parallel", "arbitrary"), vmem_limit_bytes=VMEM_LIMIT),
        name="gla_proj",
    )(h, gain.reshape(1, d), w_main, w_low, w_up, b_gate.reshape(1, dk))


def _gla_core_body(q_ref, k_ref, v_ref, r_ref, la_ref, gout_ref, tril_ref, o_ref, st_ref, *, seq, hk):
    c = GLA_CHUNK
    lead = seq % c
    n_chunks = seq // c
    scale = hk ** -0.5
    st_ref[...] = jnp.zeros_like(st_ref)
    row = lax.broadcasted_iota(jnp.int32, (c, c), 0)
    col = lax.broadcasted_iota(jnp.int32, (c, c), 1)
    causal = col <= row

    def chunk(q, k, v, r, a):
        hi, lo = _split_hi_lo(a)
        b = jnp.dot(tril_ref[...], jnp.concatenate([hi, lo], axis=0), preferred_element_type=F32)
        b_last = b[c - 1:c, :]
        qd = (q * scale * jnp.exp(b)).astype(BF16)
        kd = (k * jnp.exp(-b)).astype(BF16)
        ks = (k * jnp.exp(b_last - b)).astype(BF16)
        att = lax.dot_general(qd, kd, _NT, preferred_element_type=F32)
        att = jnp.where(causal, att, 0.0).astype(BF16)
        st = st_ref[...]
        o = jnp.dot(att, v.astype(BF16), preferred_element_type=F32)
        o = o + lax.dot_general(qd, st.astype(BF16), _NT, preferred_element_type=F32)
        kv = jnp.dot(v.T.astype(BF16), ks, preferred_element_type=F32)
        st_ref[...] = st * jnp.exp(b_last) + kv
        o = o * lax.rsqrt(jnp.mean(o * o, axis=-1, keepdims=True) + EPS) * gout_ref[...]
        return (o * (r * jax.nn.sigmoid(r))).astype(o_ref.dtype)

    if lead:
        def padded(ref):
            x = ref[0, 0:lead, :]
            return jnp.concatenate([jnp.zeros((c - lead, x.shape[1]), x.dtype), x], axis=0)
        o0 = chunk(padded(q_ref), padded(k_ref), padded(v_ref), padded(r_ref), padded(la_ref))
        o_ref[0, 0:lead, :] = o0[c - lead:, :]

    def step(i, carry):
        t0 = pl.multiple_of(lead + i * c, 16)
        rows = pl.ds(t0, c)
        o_ref[0, rows, :] = chunk(q_ref[0, rows, :], k_ref[0, rows, :], v_ref[0, rows, :],
                                  r_ref[0, rows, :], la_ref[0, rows, :])
        return carry

    lax.fori_loop(0, n_chunks, step, 0)


def _gla_core(proj, la, g_out, batch, seq, dk, dv):
    hk, hv = dk // GLA_HEADS, dv // GLA_HEADS
    assert hk % LANES == 0 and hv % LANES == 0 and seq % 16 == 0 and (seq % GLA_CHUNK) % 16 == 0
    proj = proj.reshape(batch, seq, proj.shape[1])
    la = la.reshape(batch, seq, dk)
    tril = (jnp.arange(GLA_CHUNK)[:, None] >= jnp.arange(2 * GLA_CHUNK)[None, :] % GLA_CHUNK).astype(BF16)
    body = functools.partial(_gla_core_body, seq=seq, hk=hk)
    nk, nv = dk // hk, dv // hv
    out = pl.pallas_call(
        body,
        out_shape=jax.ShapeDtypeStruct((batch, seq, dv), BF16),
        grid=(batch, GLA_HEADS),
        in_specs=[
            pl.BlockSpec((1, seq, hk), lambda b, h: (b, 0, h)),
            pl.BlockSpec((1, seq, hk), lambda b, h: (b, 0, nk + h)),
            pl.BlockSpec((1, seq, hv), lambda b, h: (b, 0, (2 * dk) // hv + h)),
            pl.BlockSpec((1, seq, hv), lambda b, h: (b, 0, (2 * dk) // hv + nv + h)),
            pl.BlockSpec((1, seq, hk), lambda b, h: (b, 0, h)),
            pl.BlockSpec((1, hv), lambda b, h: (0, h)),
            pl.BlockSpec((GLA_CHUNK, 2 * GLA_CHUNK), lambda b, h: (0, 0)),
        ],
        out_specs=pl.BlockSpec((1, seq, hv), lambda b, h: (b, 0, h)),
        scratch_shapes=[pltpu.VMEM((hv, hk), F32)],
        compiler_params=pltpu.CompilerParams(
            dimension_semantics=("parallel", "parallel"), vmem_limit_bytes=VMEM_LIMIT),
        name="gla_core",
    )(proj, proj, proj, proj, la, g_out.reshape(1, dv), tril)
    return out.reshape(batch * seq, dv)


def _gla_layer(h, gain, w_in, w_gate_up, b_gate, g_out, w_o, batch, seq):
    d = h.shape[1]
    dk = w_gate_up.shape[1]
    dv = g_out.shape[0]
    n_main = 2 * dk + 2 * dv
    w_main = w_in[:, :n_main]
    w_low = jnp.pad(w_in[:, n_main:], ((0, 0), (0, LANES - GLA_GATE_RANK)))
    w_up = jnp.pad(w_gate_up, ((0, LANES - GLA_GATE_RANK), (0, 0)))
    proj, la = _gla_proj(h, gain, w_main, w_low, w_up, b_gate)
    o = _gla_core(proj, la, g_out, batch, seq, dk, dv)
    return _out_proj(h, o, w_o)


def kernel(x, meta, ffn_a_norm, ffn_a_w_gu, ffn_a_w_down, mix_norm, sb_w_qkv, sb_q_norm, sb_k_norm, sb_w_o, gla_w_in, gla_w_gate_up, gla_b_gate, gla_out_norm, gla_w_o, ffn_b_norm, ffn_b_w_gu, ffn_b_w_down):
    batch, _, d = x.shape
    depth = ffn_a_norm.shape[0]
    m = jnp.broadcast_to(meta.astype(x.dtype)[None], (batch, meta.shape[0], d))
    h = jnp.concatenate([m, x], axis=1)
    seq = h.shape[1]
    h = h.reshape(batch * seq, d)
    bf = lambda w: w.astype(BF16)
    a_gu, a_down, b_gu, b_down = bf(ffn_a_w_gu), bf(ffn_a_w_down), bf(ffn_b_w_gu), bf(ffn_b_w_down)
    w_qkv, w_sbo, w_in, w_up, w_glao = bf(sb_w_qkv), bf(sb_w_o), bf(gla_w_in), bf(gla_w_gate_up), bf(gla_w_o)
    for i in range(depth):
        h = _ffn(h, ffn_a_norm[i], a_gu[i], a_down[i])
        j = i // 2
        if i % 2 == 0:
            h = _sb_layer(h, mix_norm[i], w_qkv[j], sb_q_norm[j], sb_k_norm[j], w_sbo[j], batch, seq)
        else:
            h = _gla_layer(h, mix_norm[i], w_in[j], w_up[j], gla_b_gate[j], gla_out_norm[j], w_glao[j], batch, seq)
        h = _ffn(h, ffn_b_norm[i], b_gu[i], b_down[i])
    return h.reshape(batch, seq, d)[:, meta.shape[0]:]
```

```python
import functools

import jax
import jax.numpy as jnp
from jax import lax
from jax.experimental import pallas as pl
from jax.experimental.pallas import tpu as pltpu

F32 = jnp.float32
BF16 = jnp.bfloat16

N_META = 16
RES_HALF = 0.5
EPS = 1e-6
SB_HEADS = 16
SB_HEAD_DIM = 64
GLA_HEADS = 4
GLA_GATE_RANK = 16
GLA_TAU = 16.0
GLA_CHUNK = 64

LANES = 128
SB_TILE = 128
SB_SKIP_LOG = 104.0
VMEM_LIMIT = 56 * 1024 * 1024

_NT = (((1,), (1,)), ((), ()))


def _row_tile(t):
    best = None
    for cand in range(16, min(t, 1024) + 1, 16):
        if t % cand == 0:
            best = cand
    assert best is not None, t
    return best


def _split_hi_lo(a):
    hi = a.astype(BF16)
    lo = (a - hi.astype(F32)).astype(BF16)
    return hi, lo


def _rmsnorm_rows(x, g):
    ms = jnp.mean(x * x, axis=-1, keepdims=True)
    return x * lax.rsqrt(ms + EPS) * g


def _softplus(z):
    return jnp.maximum(z, 0.0) + jnp.log(1.0 + jnp.exp(-jnp.abs(z)))


FFN_TILE = 256


def _resident(shape):
    return pl.BlockSpec(shape, lambda *_: (0,) * len(shape), pipeline_mode=pl.Buffered(1))


def _ffn_weights(w_gu, w_down):
    *lead, d, two_dff = w_gu.shape
    nf = two_dff // 2 // FFN_TILE
    assert two_dff == 2 * nf * FFN_TILE
    n = len(lead)
    gu = w_gu.astype(BF16).reshape(*lead, d, 2, nf, FFN_TILE)
    gu = gu.transpose(*range(n), n + 2, n, n + 1, n + 3).reshape(*lead, nf, d, 2 * FFN_TILE)
    return gu, w_down.astype(BF16).reshape(*lead, nf, FFN_TILE, d)


def _ffn_body(h_ref, g_ref, wgu_ref, wd_ref, o_ref, xn_ref, acc_ref):
    nf, _, two_tf = wgu_ref.shape
    tf = two_tf // 2
    xn_ref[...] = _rmsnorm_rows(h_ref[...], g_ref[...]).astype(BF16)
    acc_ref[...] = jnp.zeros_like(acc_ref)

    def hidden_tile(j, c):
        gu = jnp.dot(xn_ref[...], wgu_ref[j], preferred_element_type=F32)
        g, u = gu[:, :tf], gu[:, tf:]
        a = (g * jax.nn.sigmoid(g) * u).astype(BF16)
        acc_ref[...] += jnp.dot(a, wd_ref[j], preferred_element_type=F32)
        return c

    lax.fori_loop(0, nf, hidden_tile, 0)
    o_ref[...] = h_ref[...] + RES_HALF * acc_ref[...]


def _ffn(h, gain, w_gu, w_down):
    t, d = h.shape
    tm = _row_tile(t)
    return pl.pallas_call(
        _ffn_body,
        out_shape=jax.ShapeDtypeStruct((t, d), F32),
        grid=(t // tm,),
        in_specs=[
            pl.BlockSpec((tm, d), lambda i: (i, 0)),
            _resident((1, d)),
            _resident(w_gu.shape),
            _resident(w_down.shape),
        ],
        out_specs=pl.BlockSpec((tm, d), lambda i: (i, 0)),
        scratch_shapes=[pltpu.VMEM((tm, d), BF16), pltpu.VMEM((tm, d), F32)],
        compiler_params=pltpu.CompilerParams(
            dimension_semantics=("parallel",), vmem_limit_bytes=VMEM_LIMIT),
        name="ffn",
    )(h, gain.reshape(1, d), w_gu, w_down)


def _out_proj_body(h_ref, o_ref, w_ref, out_ref):
    out_ref[...] = h_ref[...] + jnp.dot(o_ref[...], w_ref[...], preferred_element_type=F32)


def _out_proj(h, o, w):
    t, d = h.shape
    k = o.shape[1]
    tm = _row_tile(t)
    return pl.pallas_call(
        _out_proj_body,
        out_shape=jax.ShapeDtypeStruct((t, d), F32),
        grid=(t // tm,),
        in_specs=[
            pl.BlockSpec((tm, d), lambda i: (i, 0)),
            pl.BlockSpec((tm, k), lambda i: (i, 0)),
            _resident((k, d)),
        ],
        out_specs=pl.BlockSpec((tm, d), lambda i: (i, 0)),
        compiler_params=pltpu.CompilerParams(
            dimension_semantics=("parallel",), vmem_limit_bytes=VMEM_LIMIT),
        name="out_proj",
    )(h, o, w)


def _sb_proj_body(h_ref, g_ref, w_ref, hg_ref, seg_ref, o_ref):
    n = o_ref.shape[2]
    xn = _rmsnorm_rows(h_ref[...], g_ref[...]).astype(BF16)
    seg = seg_ref[...]
    for part in range(2):
        y = jnp.dot(xn, w_ref[:, part * n:(part + 1) * n], preferred_element_type=F32)
        for c in range(n // LANES):
            yc = y[:, c * LANES:(c + 1) * LANES]
            hi, lo = _split_hi_lo(yc * yc)
            ms = jnp.dot(jnp.concatenate([hi, lo], axis=1), seg, preferred_element_type=F32)
            gain = hg_ref[part, :, c * LANES:(c + 1) * LANES]
            o_ref[part, :, c * LANES:(c + 1) * LANES] = (yc * lax.rsqrt(ms + EPS) * gain).astype(BF16)
    o_ref[2] = jnp.dot(xn, w_ref[:, 2 * n:], preferred_element_type=F32).astype(BF16)


def _sb_proj(h, gain, w_qkv, head_gains, seg):
    t, d = h.shape
    n = w_qkv.shape[1] // 3
    tm = _row_tile(t)
    return pl.pallas_call(
        _sb_proj_body,
        out_shape=jax.ShapeDtypeStruct((3, t, n), BF16),
        grid=(t // tm,),
        in_specs=[
            pl.BlockSpec((tm, d), lambda i: (i, 0)),
            _resident((1, d)),
            _resident(w_qkv.shape),
            _resident(head_gains.shape),
            _resident(seg.shape),
        ],
        out_specs=pl.BlockSpec((3, tm, n), lambda i: (0, i, 0)),
        compiler_params=pltpu.CompilerParams(
            dimension_semantics=("parallel",), vmem_limit_bytes=VMEM_LIMIT),
        name="sb_proj",
    )(h, gain.reshape(1, d), w_qkv, head_gains, seg)


def _sb_attn_body(q_ref, k_ref, v_ref, cum_ref, o_ref, qp_ref, kp_ref, vp_ref, carry_ref, acc_ref,
                  flag_ref, *, seq, group):
    tile = SB_TILE
    nw = -(-seq // tile)
    lp = nw * tile
    ng = -(-nw // group)
    big = jnp.float32(3.0e38)

    for src, dst in ((q_ref, qp_ref), (k_ref, kp_ref), (v_ref, vp_ref)):
        dst[0:seq, :] = src[0]
        if lp > seq:
            dst[seq:lp, :] = jnp.zeros((lp - seq, LANES), dst.dtype)
    carry_ref[nw] = jnp.zeros(carry_ref.shape[1:], F32)
    acc_ref[nw] = jnp.zeros(acc_ref.shape[1:], F32)

    lane = lax.broadcasted_iota(jnp.int32, (1, LANES), 1)
    head0 = lane < SB_HEAD_DIM
    row = lax.broadcasted_iota(jnp.int32, (2 * tile, tile), 0)
    col = lax.broadcasted_iota(jnp.int32, (2 * tile, tile), 1)
    causal = col < jnp.where(row >= tile, row - tile, row)

    def group_step(g, d, first):
        zs, sps, hls, vbs, slots, nexts = [], [], [], [], [], []
        for u in range(group):
            i = g * group + u
            i_c = jnp.minimum(i, nw - 1)
            valid = jnp.logical_and(i < nw, i >= d)
            slots.append(jnp.where(valid, i, nw))
            nexts.append(jnp.logical_and(i < nw, i >= d + 1))
            q = qp_ref[pl.ds(pl.multiple_of(i_c * tile, tile), tile), :]
            zero = jnp.zeros_like(q)
            q2 = jnp.concatenate([jnp.where(head0, q, zero), jnp.where(head0, zero, q)], axis=0)
            kt = pl.multiple_of(jnp.maximum(i_c - d, 0) * tile, tile)
            z = lax.dot_general(q2, kp_ref[pl.ds(kt, tile), :], _NT, preferred_element_type=F32)
            sp = _softplus(z)
            hi, lo = _split_hi_lo(jnp.where(causal, sp, 0.0) if first else sp)
            zs.append(z)
            sps.append(sp)
            hls.append(jnp.concatenate([hi, lo], axis=1))
            vbs.append(vp_ref[pl.ds(kt, tile), :])
        cr = jnp.dot(jnp.concatenate(hls, axis=0), cum_ref[...], preferred_element_type=F32)
        low = big
        for u in range(group):
            cr_u = cr[u * 2 * tile:(u + 1) * 2 * tile]
            later, total = cr_u[:, :tile], cr_u[:, tile:]
            if not first:
                carry_old = carry_ref[slots[u]]
                later, total = later + carry_old, total + carry_old
            w = jnp.exp(zs[u] - sps[u] - later)
            if first:
                w = jnp.where(causal, w, 0.0)
            acc = jnp.dot(w.astype(BF16), vbs[u], preferred_element_type=F32)
            if not first:
                acc = acc + acc_ref[slots[u]]
            carry_ref[slots[u]] = total
            acc_ref[slots[u]] = acc
            low = jnp.minimum(low, jnp.where(nexts[u], jnp.min(total), big))
        flag_ref[g] = (low < SB_SKIP_LOG).astype(jnp.int32)

    def first_group(g, count):
        group_step(g, 0, True)
        return count + flag_ref[g]

    def later_step(state):
        d, _ = state

        def one_group(g, count):
            @pl.when(flag_ref[g] != 0)
            def _():
                group_step(g, d, False)
            return count + flag_ref[g]

        return d + 1, lax.fori_loop(0, ng, one_group, jnp.int32(0))

    pending = lax.fori_loop(0, ng, first_group, jnp.int32(0))
    lax.while_loop(lambda state: state[1] > 0, later_step, (jnp.int32(1), pending))

    def emit(i, start, rows):
        a = acc_ref[i]
        out = jnp.where(head0, a[:tile], a[tile:])
        o_ref[0, pl.ds(start, rows), :] = out[:rows].astype(o_ref.dtype)

    def emit_full(i, c):
        emit(i, pl.multiple_of(i * tile, tile), tile)
        return c

    lax.fori_loop(0, seq // tile, emit_full, 0)
    if lp > seq:
        emit(nw - 1, (nw - 1) * tile, seq - (nw - 1) * tile)


def _sb_attn(qkv, cum, batch, seq, *, group=6):
    n = qkv.shape[2]
    assert seq % 16 == 0 and n % LANES == 0
    qkv = qkv.reshape(3, batch, seq, n)
    nw = -(-seq // SB_TILE)
    ng = -(-nw // group)
    body = functools.partial(_sb_attn_body, seq=seq, group=group)

    def spec(which):
        return pl.BlockSpec((None, 1, seq, LANES), lambda b, p, which=which: (which, b, 0, p))

    padded = pltpu.VMEM((nw * SB_TILE, LANES), BF16)
    state = pltpu.VMEM((nw + 1, 2 * SB_TILE, SB_TILE), F32)
    out = pl.pallas_call(
        body,
        out_shape=jax.ShapeDtypeStruct((batch, seq, n), BF16),
        grid=(batch, n // LANES),
        in_specs=[spec(0), spec(1), spec(2),
                  pl.BlockSpec((2 * SB_TILE, 2 * SB_TILE), lambda b, p: (0, 0))],
        out_specs=pl.BlockSpec((1, seq, LANES), lambda b, p: (b, 0, p)),
        scratch_shapes=[padded, padded, padded, state, state, pltpu.SMEM((ng,), jnp.int32)],
        compiler_params=pltpu.CompilerParams(
            dimension_semantics=("parallel", "parallel"), vmem_limit_bytes=VMEM_LIMIT),
        name="sb_attn",
    )(qkv, qkv, qkv, cum)
    return out.reshape(batch * seq, n)


def _sb_constants():
    r = jnp.arange(2 * LANES)[:, None] % LANES
    c = jnp.arange(LANES)[None, :]
    seg = jnp.where(r // SB_HEAD_DIM == c // SB_HEAD_DIM, 1.0 / SB_HEAD_DIM, 0.0).astype(BF16)
    j = jnp.arange(2 * SB_TILE)[:, None] % SB_TILE
    s = jnp.arange(2 * SB_TILE)[None, :]
    cum = jnp.where(jnp.logical_or(s >= SB_TILE, j > s), 1.0, 0.0).astype(BF16)
    return seg, cum


def _sb_layer(h, gain, w_qkv, g_q, g_k, w_o, batch, seq):
    seg, cum = _sb_constants()
    scale = SB_HEAD_DIM ** -0.5
    head_gains = jnp.stack([jnp.tile(g_q, SB_HEADS) * scale, jnp.tile(g_k, SB_HEADS)])[:, None, :]
    qkv = _sb_proj(h, gain, w_qkv, head_gains.astype(F32), seg)
    o = _sb_attn(qkv, cum, batch, seq)
    return _out_proj(h, o, w_o)


def _gla_proj_body(h_ref, g_ref, w_ref, wlow_ref, wup_ref, b_ref, o_ref, la_ref):
    xn = _rmsnorm_rows(h_ref[...], g_ref[...]).astype(BF16)
    g_low = jnp.dot(xn, wlow_ref[...], preferred_element_type=F32).astype(BF16)
    pre = jnp.dot(g_low, wup_ref[...], preferred_element_type=F32) + b_ref[...]
    la_ref[...] = (jnp.minimum(pre, 0.0) - jnp.log(1.0 + jnp.exp(-jnp.abs(pre)))) / GLA_TAU
    o_ref[...] = jnp.dot(xn, w_ref[...], preferred_element_type=F32)


def _gla_proj(h, gain, w_main, w_low, w_up, b_gate):
    t, d = h.shape
    n = w_main.shape[1]
    dk = w_up.shape[1]
    tm = _row_tile(t)
    return pl.pallas_call(
        _gla_proj_body,
        out_shape=(jax.ShapeDtypeStruct((t, n), F32), jax.ShapeDtypeStruct((t, dk), F32)),
        grid=(t // tm,),
        in_specs=[
            pl.BlockSpec((tm, d), lambda i: (i, 0)),
            _resident((1, d)),
            _resident(w_main.shape),
            _resident(w_low.shape),
            _resident(w_up.shape),
            _resident((1, dk)),
        ],
        out_specs=(pl.BlockSpec((tm, n), lambda i: (i, 0)),
                   pl.BlockSpec((tm, dk), lambda i: (i, 0))),
        compiler_params=pltpu.CompilerParams(
            dimension_semantics=("parallel",), vmem_limit_bytes=VMEM_LIMIT),
        name="gla_proj",
    )(h, gain.reshape(1, d), w_main, w_low, w_up, b_gate.reshape(1, dk))


def _gla_core_body(q_ref, k_ref, v_ref, r_ref, la_ref, gout_ref, tril_ref, o_ref,
                   st_ref, oi_ref, kv_ref, qd_ref, dec_ref, *, seq, hk, group):
    c = GLA_CHUNK
    lead = seq % c
    n_full = seq // c
    first = 1 if lead else 0
    scale = hk ** -0.5
    row = lax.broadcasted_iota(jnp.int32, (c, c), 0)
    col = lax.broadcasted_iota(jnp.int32, (c, c), 1)
    causal = col <= row

    def local(chunks):
        decay = []
        for _, _, _, _, a in chunks:
            hi, lo = _split_hi_lo(a)
            decay.append(jnp.dot(tril_ref[...], jnp.concatenate([hi, lo], axis=0),
                                 preferred_element_type=F32))
        atts, kss = [], []
        for (slot, q, k, _, _), b in zip(chunks, decay):
            b_last = b[c - 1:c, :]
            qd = (q * scale * jnp.exp(b)).astype(BF16)
            kd = (k * jnp.exp(-b)).astype(BF16)
            kss.append((k * jnp.exp(b_last - b)).astype(BF16))
            atts.append(lax.dot_general(qd, kd, _NT, preferred_element_type=F32))
            qd_ref[slot] = qd
            dec_ref[slot] = jnp.broadcast_to(jnp.exp(b_last), dec_ref.shape[1:])
        for (slot, _, _, v, _), att, ks in zip(chunks, atts, kss):
            att = jnp.where(causal, att, 0.0).astype(BF16)
            oi_ref[slot] = jnp.dot(att, v.astype(BF16), preferred_element_type=F32)
            kv_ref[slot] = jnp.dot(v.T.astype(BF16), ks, preferred_element_type=F32)

    def carried(slot, r):
        st = st_ref[...]
        o = oi_ref[slot] + lax.dot_general(qd_ref[slot], st.astype(BF16), _NT, preferred_element_type=F32)
        st_ref[...] = st * dec_ref[slot][0:1, :] + kv_ref[slot]
        o = o * lax.rsqrt(jnp.mean(o * o, axis=-1, keepdims=True) + EPS) * gout_ref[...]
        return (o * (r * jax.nn.sigmoid(r))).astype(o_ref.dtype)

    def padded(ref):
        x = ref[0, 0:lead, :]
        return jnp.concatenate([jnp.zeros((c - lead, x.shape[1]), x.dtype), x], axis=0)

    def rows_of(i):
        return pl.ds(pl.multiple_of(lead + i * c, 16), c)

    if lead:
        local([(0, padded(q_ref), padded(k_ref), padded(v_ref), padded(la_ref))])

    def local_group(gi, carry):
        chunks = []
        for u in range(group):
            i = gi * group + u
            rows = rows_of(i)
            chunks.append((first + i, q_ref[0, rows, :], k_ref[0, rows, :], v_ref[0, rows, :],
                           la_ref[0, rows, :]))
        local(chunks)
        return carry

    assert n_full % group == 0
    lax.fori_loop(0, n_full // group, local_group, 0)

    st_ref[...] = jnp.zeros_like(st_ref)
    if lead:
        o_ref[0, 0:lead, :] = carried(0, padded(r_ref))[c - lead:, :]

    def carried_step(i, carry):
        rows = rows_of(i)
        o_ref[0, rows, :] = carried(first + i, r_ref[0, rows, :])
        return carry

    lax.fori_loop(0, n_full, carried_step, 0, unroll=group)


def _gla_core(proj, la, g_out, batch, seq, dk, dv, *, group=8):
    hk, hv = dk // GLA_HEADS, dv // GLA_HEADS
    assert hk % LANES == 0 and hv % LANES == 0 and seq % 16 == 0 and (seq % GLA_CHUNK) % 16 == 0
    proj = proj.reshape(batch, seq, proj.shape[1])
    la = la.reshape(batch, seq, dk)
    tril = (jnp.arange(GLA_CHUNK)[:, None] >= jnp.arange(2 * GLA_CHUNK)[None, :] % GLA_CHUNK).astype(BF16)
    slots = -(-seq // GLA_CHUNK)
    group = group if (seq // GLA_CHUNK) % group == 0 else 1
    body = functools.partial(_gla_core_body, seq=seq, hk=hk, group=group)
    nk, nv = dk // hk, dv // hv
    out = pl.pallas_call(
        body,
        out_shape=jax.ShapeDtypeStruct((batch, seq, dv), BF16),
        grid=(batch, GLA_HEADS),
        in_specs=[
            pl.BlockSpec((1, seq, hk), lambda b, h: (b, 0, h)),
            pl.BlockSpec((1, seq, hk), lambda b, h: (b, 0, nk + h)),
            pl.BlockSpec((1, seq, hv), lambda b, h: (b, 0, (2 * dk) // hv + h)),
            pl.BlockSpec((1, seq, hv), lambda b, h: (b, 0, (2 * dk) // hv + nv + h)),
            pl.BlockSpec((1, seq, hk), lambda b, h: (b, 0, h)),
            pl.BlockSpec((1, hv), lambda b, h: (0, h)),
            pl.BlockSpec((GLA_CHUNK, 2 * GLA_CHUNK), lambda b, h: (0, 0)),
        ],
        out_specs=pl.BlockSpec((1, seq, hv), lambda b, h: (b, 0, h)),
        scratch_shapes=[
            pltpu.VMEM((hv, hk), F32),
            pltpu.VMEM((slots, GLA_CHUNK, hv), F32),
            pltpu.VMEM((slots, hv, hk), F32),
            pltpu.VMEM((slots, GLA_CHUNK, hk), BF16),
            pltpu.VMEM((slots, 8, hk), F32),
        ],
        compiler_params=pltpu.CompilerParams(
            dimension_semantics=("parallel", "parallel"), vmem_limit_bytes=VMEM_LIMIT),
        name="gla_core",
    )(proj, proj, proj, proj, la, g_out.reshape(1, dv), tril)
    return out.reshape(batch * seq, dv)


def _gla_layer(h, gain, w_in, w_gate_up, b_gate, g_out, w_o, batch, seq):
    d = h.shape[1]
    dk = w_gate_up.shape[1]
    dv = g_out.shape[0]
    n_main = 2 * dk + 2 * dv
    w_main = w_in[:, :n_main]
    w_low = jnp.pad(w_in[:, n_main:], ((0, 0), (0, LANES - GLA_GATE_RANK)))
    w_up = jnp.pad(w_gate_up, ((0, LANES - GLA_GATE_RANK), (0, 0)))
    proj, la = _gla_proj(h, gain, w_main, w_low, w_up, b_gate)
    o = _gla_core(proj, la, g_out, batch, seq, dk, dv)
    return _out_proj(h, o, w_o)


def kernel(x, meta, ffn_a_norm, ffn_a_w_gu, ffn_a_w_down, mix_norm, sb_w_qkv, sb_q_norm, sb_k_norm, sb_w_o, gla_w_in, gla_w_gate_up, gla_b_gate, gla_out_norm, gla_w_o, ffn_b_norm, ffn_b_w_gu, ffn_b_w_down):
    batch, _, d = x.shape
    depth = ffn_a_norm.shape[0]
    m = jnp.broadcast_to(meta.astype(x.dtype)[None], (batch, meta.shape[0], d))
    h = jnp.concatenate([m, x], axis=1)
    seq = h.shape[1]
    h = h.reshape(batch * seq, d)
    bf = lambda w: w.astype(BF16)
    a_gu, a_down = _ffn_weights(ffn_a_w_gu, ffn_a_w_down)
    b_gu, b_down = _ffn_weights(ffn_b_w_gu, ffn_b_w_down)
    w_qkv, w_sbo, w_in, w_up, w_glao = bf(sb_w_qkv), bf(sb_w_o), bf(gla_w_in), bf(gla_w_gate_up), bf(gla_w_o)
    for i in range(depth):
        h = _ffn(h, ffn_a_norm[i], a_gu[i], a_down[i])
        j = i // 2
        if i % 2 == 0:
            h = _sb_layer(h, mix_norm[i], w_qkv[j], sb_q_norm[j], sb_k_norm[j], w_sbo[j], batch, seq)
        else:
            h = _gla_layer(h, mix_norm[i], w_in[j], w_up[j], gla_b_gate[j], gla_out_norm[j], w_glao[j], batch, seq)
        h = _ffn(h, ffn_b_norm[i], b_gu[i], b_down[i])
    return h.reshape(batch, seq, d)[:, meta.shape[0]:]
```

```python
import functools

import jax
import jax.numpy as jnp
from jax import lax
from jax.experimental import pallas as pl
from jax.experimental.pallas import tpu as pltpu

F32 = jnp.float32
BF16 = jnp.bfloat16

N_META = 16
RES_HALF = 0.5
EPS = 1e-6
SB_HEADS = 16
SB_HEAD_DIM = 64
GLA_HEADS = 4
GLA_GATE_RANK = 16
GLA_TAU = 16.0
GLA_CHUNK = 64

LANES = 128
SB_TILE = 128
SB_SKIP_LOG = 104.0
VMEM_LIMIT = 56 * 1024 * 1024

_NT = (((1,), (1,)), ((), ()))


def _row_tile(t):
    best = None
    for cand in range(16, min(t, 1024) + 1, 16):
        if t % cand == 0:
            best = cand
    assert best is not None, t
    return best


def _split_hi_lo(a):
    hi = a.astype(BF16)
    lo = (a - hi.astype(F32)).astype(BF16)
    return hi, lo


def _rmsnorm_rows(x, g):
    ms = jnp.mean(x * x, axis=-1, keepdims=True)
    return x * lax.rsqrt(ms + EPS) * g


def _softplus(z):
    return jnp.maximum(z, 0.0) + jnp.log(1.0 + jnp.exp(-jnp.abs(z)))


FFN_TILE = 256


def _resident(shape, layer=None):
    if layer is None:
        return pl.BlockSpec(shape, lambda *_: (0,) * len(shape), pipeline_mode=pl.Buffered(1))
    return pl.BlockSpec((None, *shape[1:]), lambda *_: (layer,) + (0,) * (len(shape) - 1),
                        pipeline_mode=pl.Buffered(1))


def _ffn_body(h_ref, g_ref, wgu_ref, wd_ref, o_ref, xn_ref, acc_ref):
    dff = wd_ref.shape[0]
    tf = FFN_TILE
    nf = dff // tf
    xn_ref[...] = _rmsnorm_rows(h_ref[...], g_ref[...]).astype(BF16)
    for j in range(nf):
        xn = xn_ref[...]
        g = jnp.dot(xn, wgu_ref[:, j * tf:(j + 1) * tf], preferred_element_type=F32)
        u = jnp.dot(xn, wgu_ref[:, dff + j * tf:dff + (j + 1) * tf], preferred_element_type=F32)
        a = (g * jax.nn.sigmoid(g) * u).astype(BF16)
        part = jnp.dot(a, wd_ref[j * tf:(j + 1) * tf, :], preferred_element_type=F32)
        if j == 0:
            acc_ref[...] = part
        elif j < nf - 1:
            acc_ref[...] += part
        else:
            o_ref[...] = h_ref[...] + RES_HALF * (acc_ref[...] + part)


def _ffn(h, gain, w_gu, w_down, layer):
    t, d = h.shape
    assert w_down.shape[1] % FFN_TILE == 0
    tm = _row_tile(t)
    return pl.pallas_call(
        _ffn_body,
        out_shape=jax.ShapeDtypeStruct((t, d), F32),
        grid=(t // tm,),
        in_specs=[
            pl.BlockSpec((tm, d), lambda i: (i, 0)),
            _resident((1, d)),
            _resident(w_gu.shape, layer),
            _resident(w_down.shape, layer),
        ],
        out_specs=pl.BlockSpec((tm, d), lambda i: (i, 0)),
        scratch_shapes=[pltpu.VMEM((tm, d), BF16), pltpu.VMEM((tm, d), F32)],
        compiler_params=pltpu.CompilerParams(
            dimension_semantics=("parallel",), vmem_limit_bytes=VMEM_LIMIT),
        name="ffn",
    )(h, gain.reshape(1, d), w_gu, w_down)


def _out_proj_body(h_ref, o_ref, w_ref, out_ref):
    out_ref[...] = h_ref[...] + jnp.dot(o_ref[...], w_ref[...], preferred_element_type=F32)


def _out_proj(h, o, w, layer):
    t, d = h.shape
    k = o.shape[1]
    tm = _row_tile(t)
    return pl.pallas_call(
        _out_proj_body,
        out_shape=jax.ShapeDtypeStruct((t, d), F32),
        grid=(t // tm,),
        in_specs=[
            pl.BlockSpec((tm, d), lambda i: (i, 0)),
            pl.BlockSpec((tm, k), lambda i: (i, 0)),
            _resident(w.shape, layer),
        ],
        out_specs=pl.BlockSpec((tm, d), lambda i: (i, 0)),
        compiler_params=pltpu.CompilerParams(
            dimension_semantics=("parallel",), vmem_limit_bytes=VMEM_LIMIT),
        name="out_proj",
    )(h, o, w)


def _sb_proj_body(h_ref, g_ref, w_ref, hg_ref, seg_ref, o_ref):
    n = o_ref.shape[2]
    xn = _rmsnorm_rows(h_ref[...], g_ref[...]).astype(BF16)
    seg = seg_ref[...]
    for part in range(2):
        y = jnp.dot(xn, w_ref[:, part * n:(part + 1) * n], preferred_element_type=F32)
        for c in range(n // LANES):
            yc = y[:, c * LANES:(c + 1) * LANES]
            hi, lo = _split_hi_lo(yc * yc)
            ms = jnp.dot(jnp.concatenate([hi, lo], axis=1), seg, preferred_element_type=F32)
            gain = hg_ref[part, :, c * LANES:(c + 1) * LANES]
            o_ref[part, :, c * LANES:(c + 1) * LANES] = (yc * lax.rsqrt(ms + EPS) * gain).astype(BF16)
    o_ref[2] = jnp.dot(xn, w_ref[:, 2 * n:], preferred_element_type=F32).astype(BF16)


def _sb_proj(h, gain, w_qkv, layer, head_gains, seg):
    t, d = h.shape
    n = w_qkv.shape[2] // 3
    tm = _row_tile(t)
    return pl.pallas_call(
        _sb_proj_body,
        out_shape=jax.ShapeDtypeStruct((3, t, n), BF16),
        grid=(t // tm,),
        in_specs=[
            pl.BlockSpec((tm, d), lambda i: (i, 0)),
            _resident((1, d)),
            _resident(w_qkv.shape, layer),
            _resident(head_gains.shape),
            _resident(seg.shape),
        ],
        out_specs=pl.BlockSpec((3, tm, n), lambda i: (0, i, 0)),
        compiler_params=pltpu.CompilerParams(
            dimension_semantics=("parallel",), vmem_limit_bytes=VMEM_LIMIT),
        name="sb_proj",
    )(h, gain.reshape(1, d), w_qkv, head_gains, seg)


def _sb_attn_body(q_ref, k_ref, v_ref, cum_ref, o_ref, qp_ref, kp_ref, vp_ref, carry_ref, acc_ref,
                  flag_ref, *, seq, group):
    tile = SB_TILE
    nw = -(-seq // tile)
    lp = nw * tile
    ng = -(-nw // group)
    big = jnp.float32(3.0e38)

    for src, dst in ((q_ref, qp_ref), (k_ref, kp_ref), (v_ref, vp_ref)):
        dst[0:seq, :] = src[0]
        if lp > seq:
            dst[seq:lp, :] = jnp.zeros((lp - seq, LANES), dst.dtype)
    carry_ref[nw] = jnp.zeros(carry_ref.shape[1:], F32)
    acc_ref[nw] = jnp.zeros(acc_ref.shape[1:], F32)

    lane = lax.broadcasted_iota(jnp.int32, (1, LANES), 1)
    head0 = lane < SB_HEAD_DIM
    row = lax.broadcasted_iota(jnp.int32, (2 * tile, tile), 0)
    col = lax.broadcasted_iota(jnp.int32, (2 * tile, tile), 1)
    causal = col < jnp.where(row >= tile, row - tile, row)

    def group_step(g, d, first):
        zs, sps, hls, vbs, slots, nexts = [], [], [], [], [], []
        for u in range(group):
            i = g * group + u
            i_c = jnp.minimum(i, nw - 1)
            valid = jnp.logical_and(i < nw, i >= d)
            slots.append(jnp.where(valid, i, nw))
            nexts.append(jnp.logical_and(i < nw, i >= d + 1))
            q = qp_ref[pl.ds(pl.multiple_of(i_c * tile, tile), tile), :]
            zero = jnp.zeros_like(q)
            q2 = jnp.concatenate([jnp.where(head0, q, zero), jnp.where(head0, zero, q)], axis=0)
            kt = pl.multiple_of(jnp.maximum(i_c - d, 0) * tile, tile)
            z = lax.dot_general(q2, kp_ref[pl.ds(kt, tile), :], _NT, preferred_element_type=F32)
            sp = _softplus(z)
            hi, lo = _split_hi_lo(jnp.where(causal, sp, 0.0) if first else sp)
            zs.append(z)
            sps.append(sp)
            hls.append(jnp.concatenate([hi, lo], axis=1))
            vbs.append(vp_ref[pl.ds(kt, tile), :])
        cr = jnp.dot(jnp.concatenate(hls, axis=0), cum_ref[...], preferred_element_type=F32)
        low = big
        for u in range(group):
            cr_u = cr[u * 2 * tile:(u + 1) * 2 * tile]
            later, total = cr_u[:, :tile], cr_u[:, tile:]
            if not first:
                carry_old = carry_ref[slots[u]]
                later, total = later + carry_old, total + carry_old
            w = jnp.exp(zs[u] - sps[u] - later)
            if first:
                w = jnp.where(causal, w, 0.0)
            acc = jnp.dot(w.astype(BF16), vbs[u], preferred_element_type=F32)
            if not first:
                acc = acc + acc_ref[slots[u]]
            carry_ref[slots[u]] = total
            acc_ref[slots[u]] = acc
            low = jnp.minimum(low, jnp.where(nexts[u], jnp.min(total), big))
        flag_ref[g] = (low < SB_SKIP_LOG).astype(jnp.int32)

    def first_group(g, count):
        group_step(g, 0, True)
        return count + flag_ref[g]

    def later_step(state):
        d, _ = state

        def one_group(g, count):
            @pl.when(flag_ref[g] != 0)
            def _():
                group_step(g, d, False)
            return count + flag_ref[g]

        return d + 1, lax.fori_loop(0, ng, one_group, jnp.int32(0))

    pending = lax.fori_loop(0, ng, first_group, jnp.int32(0))
    lax.while_loop(lambda state: state[1] > 0, later_step, (jnp.int32(1), pending))

    def emit(i, start, rows):
        a = acc_ref[i]
        out = jnp.where(head0, a[:tile], a[tile:])
        o_ref[0, pl.ds(start, rows), :] = out[:rows].astype(o_ref.dtype)

    def emit_full(i, c):
        emit(i, pl.multiple_of(i * tile, tile), tile)
        return c

    lax.fori_loop(0, seq // tile, emit_full, 0)
    if lp > seq:
        emit(nw - 1, (nw - 1) * tile, seq - (nw - 1) * tile)


def _sb_attn(qkv, cum, batch, seq, *, group=6):
    n = qkv.shape[2]
    assert seq % 16 == 0 and n % LANES == 0
    qkv = qkv.reshape(3, batch, seq, n)
    nw = -(-seq // SB_TILE)
    ng = -(-nw // group)
    body = functools.partial(_sb_attn_body, seq=seq, group=group)

    def spec(which):
        return pl.BlockSpec((None, 1, seq, LANES), lambda b, p, which=which: (which, b, 0, p))

    padded = pltpu.VMEM((nw * SB_TILE, LANES), BF16)
    state = pltpu.VMEM((nw + 1, 2 * SB_TILE, SB_TILE), F32)
    out = pl.pallas_call(
        body,
        out_shape=jax.ShapeDtypeStruct((batch, seq, n), BF16),
        grid=(batch, n // LANES),
        in_specs=[spec(0), spec(1), spec(2),
                  pl.BlockSpec((2 * SB_TILE, 2 * SB_TILE), lambda b, p: (0, 0))],
        out_specs=pl.BlockSpec((1, seq, LANES), lambda b, p: (b, 0, p)),
        scratch_shapes=[padded, padded, padded, state, state, pltpu.SMEM((ng,), jnp.int32)],
        compiler_params=pltpu.CompilerParams(
            dimension_semantics=("parallel", "parallel"), vmem_limit_bytes=VMEM_LIMIT),
        name="sb_attn",
    )(qkv, qkv, qkv, cum)
    return out.reshape(batch * seq, n)


def _sb_constants():
    r = jnp.arange(2 * LANES)[:, None] % LANES
    c = jnp.arange(LANES)[None, :]
    seg = jnp.where(r // SB_HEAD_DIM == c // SB_HEAD_DIM, 1.0 / SB_HEAD_DIM, 0.0).astype(BF16)
    j = jnp.arange(2 * SB_TILE)[:, None] % SB_TILE
    s = jnp.arange(2 * SB_TILE)[None, :]
    cum = jnp.where(jnp.logical_or(s >= SB_TILE, j > s), 1.0, 0.0).astype(BF16)
    return seg, cum


def _sb_layer(h, gain, w_qkv, g_q, g_k, w_o, layer, batch, seq):
    seg, cum = _sb_constants()
    scale = SB_HEAD_DIM ** -0.5
    head_gains = jnp.stack([jnp.tile(g_q, SB_HEADS) * scale, jnp.tile(g_k, SB_HEADS)])[:, None, :]
    qkv = _sb_proj(h, gain, w_qkv, layer, head_gains.astype(F32), seg)
    o = _sb_attn(qkv, cum, batch, seq)
    return _out_proj(h, o, w_o, layer)


def _gla_proj_body(h_ref, g_ref, w_ref, wlow_ref, wup_ref, b_ref, o_ref, la_ref):
    xn = _rmsnorm_rows(h_ref[...], g_ref[...]).astype(BF16)
    g_low = jnp.dot(xn, wlow_ref[...], preferred_element_type=F32).astype(BF16)
    pre = jnp.dot(g_low, wup_ref[...], preferred_element_type=F32) + b_ref[...]
    la_ref[...] = (jnp.minimum(pre, 0.0) - jnp.log(1.0 + jnp.exp(-jnp.abs(pre)))) / GLA_TAU
    o_ref[...] = jnp.dot(xn, w_ref[:, :o_ref.shape[1]], preferred_element_type=F32)


def _gla_proj(h, gain, w_in, w_low, w_up, layer, b_gate):
    t, d = h.shape
    dk = w_up.shape[2]
    n = w_in.shape[2] - GLA_GATE_RANK
    tm = _row_tile(t)
    return pl.pallas_call(
        _gla_proj_body,
        out_shape=(jax.ShapeDtypeStruct((t, n), F32), jax.ShapeDtypeStruct((t, dk), F32)),
        grid=(t // tm,),
        in_specs=[
            pl.BlockSpec((tm, d), lambda i: (i, 0)),
            _resident((1, d)),
            _resident(w_in.shape, layer),
            _resident(w_low.shape, layer),
            _resident(w_up.shape, layer),
            _resident((1, dk)),
        ],
        out_specs=(pl.BlockSpec((tm, n), lambda i: (i, 0)),
                   pl.BlockSpec((tm, dk), lambda i: (i, 0))),
        compiler_params=pltpu.CompilerParams(
            dimension_semantics=("parallel",), vmem_limit_bytes=VMEM_LIMIT),
        name="gla_proj",
    )(h, gain.reshape(1, d), w_in, w_low, w_up, b_gate.reshape(1, dk))


def _gla_core_body(q_ref, k_ref, v_ref, r_ref, la_ref, gout_ref, tril_ref, o_ref,
                   st_ref, oi_ref, kv_ref, qd_ref, dec_ref, *, seq, hk, group):
    c = GLA_CHUNK
    lead = seq % c
    n_full = seq // c
    first = 1 if lead else 0
    scale = hk ** -0.5
    row = lax.broadcasted_iota(jnp.int32, (c, c), 0)
    col = lax.broadcasted_iota(jnp.int32, (c, c), 1)
    causal = col <= row

    def local(chunks):
        decay = []
        for _, _, _, _, a in chunks:
            hi, lo = _split_hi_lo(a)
            decay.append(jnp.dot(tril_ref[...], jnp.concatenate([hi, lo], axis=0),
                                 preferred_element_type=F32))
        atts, kss = [], []
        for (slot, q, k, _, _), b in zip(chunks, decay):
            b_last = b[c - 1:c, :]
            qd = (q * scale * jnp.exp(b)).astype(BF16)
            kd = (k * jnp.exp(-b)).astype(BF16)
            kss.append((k * jnp.exp(b_last - b)).astype(BF16))
            atts.append(lax.dot_general(qd, kd, _NT, preferred_element_type=F32))
            qd_ref[slot] = qd
            dec_ref[slot] = jnp.broadcast_to(jnp.exp(b_last), dec_ref.shape[1:])
        for (slot, _, _, v, _), att, ks in zip(chunks, atts, kss):
            att = jnp.where(causal, att, 0.0).astype(BF16)
            oi_ref[slot] = jnp.dot(att, v.astype(BF16), preferred_element_type=F32)
            kv_ref[slot] = jnp.dot(v.T.astype(BF16), ks, preferred_element_type=F32)

    def carried(slot, r):
        st = st_ref[...]
        o = oi_ref[slot] + lax.dot_general(qd_ref[slot], st.astype(BF16), _NT, preferred_element_type=F32)
        st_ref[...] = st * dec_ref[slot][0:1, :] + kv_ref[slot]
        o = o * lax.rsqrt(jnp.mean(o * o, axis=-1, keepdims=True) + EPS) * gout_ref[...]
        return (o * (r * jax.nn.sigmoid(r))).astype(o_ref.dtype)

    def padded(ref):
        x = ref[0, 0:lead, :]
        return jnp.concatenate([jnp.zeros((c - lead, x.shape[1]), x.dtype), x], axis=0)

    def rows_of(i):
        return pl.ds(pl.multiple_of(lead + i * c, 16), c)

    if lead:
        local([(0, padded(q_ref), padded(k_ref), padded(v_ref), padded(la_ref))])

    def local_group(gi, carry):
        chunks = []
        for u in range(group):
            i = gi * group + u
            rows = rows_of(i)
            chunks.append((first + i, q_ref[0, rows, :], k_ref[0, rows, :], v_ref[0, rows, :],
                           la_ref[0, rows, :]))
        local(chunks)
        return carry

    assert n_full % group == 0
    lax.fori_loop(0, n_full // group, local_group, 0)

    st_ref[...] = jnp.zeros_like(st_ref)
    if lead:
        o_ref[0, 0:lead, :] = carried(0, padded(r_ref))[c - lead:, :]

    def carried_step(i, carry):
        rows = rows_of(i)
        o_ref[0, rows, :] = carried(first + i, r_ref[0, rows, :])
        return carry

    lax.fori_loop(0, n_full, carried_step, 0, unroll=group)


def _gla_core(proj, la, g_out, batch, seq, dk, dv, *, group=8):
    hk, hv = dk // GLA_HEADS, dv // GLA_HEADS
    assert hk % LANES == 0 and hv % LANES == 0 and seq % 16 == 0 and (seq % GLA_CHUNK) % 16 == 0
    proj = proj.reshape(batch, seq, proj.shape[1])
    la = la.reshape(batch, seq, dk)
    tril = (jnp.arange(GLA_CHUNK)[:, None] >= jnp.arange(2 * GLA_CHUNK)[None, :] % GLA_CHUNK).astype(BF16)
    slots = -(-seq // GLA_CHUNK)
    group = group if (seq // GLA_CHUNK) % group == 0 else 1
    body = functools.partial(_gla_core_body, seq=seq, hk=hk, group=group)
    nk, nv = dk // hk, dv // hv
    out = pl.pallas_call(
        body,
        out_shape=jax.ShapeDtypeStruct((batch, seq, dv), BF16),
        grid=(batch, GLA_HEADS),
        in_specs=[
            pl.BlockSpec((1, seq, hk), lambda b, h: (b, 0, h)),
            pl.BlockSpec((1, seq, hk), lambda b, h: (b, 0, nk + h)),
            pl.BlockSpec((1, seq, hv), lambda b, h: (b, 0, (2 * dk) // hv + h)),
            pl.BlockSpec((1, seq, hv), lambda b, h: (b, 0, (2 * dk) // hv + nv + h)),
            pl.BlockSpec((1, seq, hk), lambda b, h: (b, 0, h)),
            pl.BlockSpec((1, hv), lambda b, h: (0, h)),
            pl.BlockSpec((GLA_CHUNK, 2 * GLA_CHUNK), lambda b, h: (0, 0)),
        ],
        out_specs=pl.BlockSpec((1, seq, hv), lambda b, h: (b, 0, h)),
        scratch_shapes=[
            pltpu.VMEM((hv, hk), F32),
            pltpu.VMEM((slots, GLA_CHUNK, hv), F32),
            pltpu.VMEM((slots, hv, hk), F32),
            pltpu.VMEM((slots, GLA_CHUNK, hk), BF16),
            pltpu.VMEM((slots, 8, hk), F32),
        ],
        compiler_params=pltpu.CompilerParams(
            dimension_semantics=("parallel", "parallel"), vmem_limit_bytes=VMEM_LIMIT),
        name="gla_core",
    )(proj, proj, proj, proj, la, g_out.reshape(1, dv), tril)
    return out.reshape(batch * seq, dv)


def _gla_layer(h, gain, w_in, w_low, w_up, b_gate, g_out, w_o, layer, batch, seq):
    dk = w_up.shape[2]
    dv = g_out.shape[0]
    proj, la = _gla_proj(h, gain, w_in, w_low, w_up, layer, b_gate)
    o = _gla_core(proj, la, g_out, batch, seq, dk, dv)
    return _out_proj(h, o, w_o, layer)


def kernel(x, meta, ffn_a_norm, ffn_a_w_gu, ffn_a_w_down, mix_norm, sb_w_qkv, sb_q_norm, sb_k_norm, sb_w_o, gla_w_in, gla_w_gate_up, gla_b_gate, gla_out_norm, gla_w_o, ffn_b_norm, ffn_b_w_gu, ffn_b_w_down):
    batch, _, d = x.shape
    depth = ffn_a_norm.shape[0]
    m = jnp.broadcast_to(meta.astype(x.dtype)[None], (batch, meta.shape[0], d))
    h = jnp.concatenate([m, x], axis=1)
    seq = h.shape[1]
    h = h.reshape(batch * seq, d)
    bf = lambda w: w.astype(BF16)
    a_gu, a_down, b_gu, b_down = bf(ffn_a_w_gu), bf(ffn_a_w_down), bf(ffn_b_w_gu), bf(ffn_b_w_down)
    w_qkv, w_sbo, w_in, w_glao = bf(sb_w_qkv), bf(sb_w_o), bf(gla_w_in), bf(gla_w_o)
    low_pad = LANES - GLA_GATE_RANK
    w_low = jnp.pad(w_in[:, :, w_in.shape[2] - GLA_GATE_RANK:], ((0, 0), (0, 0), (0, low_pad)))
    w_up = jnp.pad(bf(gla_w_gate_up), ((0, 0), (0, low_pad), (0, 0)))
    for i in range(depth):
        h = _ffn(h, ffn_a_norm[i], a_gu, a_down, i)
        j = i // 2
        if i % 2 == 0:
            h = _sb_layer(h, mix_norm[i], w_qkv, sb_q_norm[j], sb_k_norm[j], w_sbo, j, batch, seq)
        else:
            h = _gla_layer(h, mix_norm[i], w_in, w_low, w_up, gla_b_gate[j], gla_out_norm[j], w_glao, j,
                           batch, seq)
        h = _ffn(h, ffn_b_norm[i], b_gu, b_down, i)
    return h.reshape(batch, seq, d)[:, meta.shape[0]:]
```

```python
import functools

import jax
import jax.numpy as jnp
from jax import lax
from jax.experimental import pallas as pl
from jax.experimental.pallas import tpu as pltpu

F32 = jnp.float32
BF16 = jnp.bfloat16

N_META = 16
RES_HALF = 0.5
EPS = 1e-6
SB_HEADS = 16
SB_HEAD_DIM = 64
GLA_HEADS = 4
GLA_GATE_RANK = 16
GLA_TAU = 16.0
GLA_CHUNK = 64

LANES = 128
SB_ROWS = 64
SB_TILE = 128
SB_SKIP_LOG = 104.0
VMEM_LIMIT = 56 * 1024 * 1024

_NT = (((1,), (1,)), ((), ()))


def _row_tile(t):
    best = None
    for cand in range(16, min(t, 1024) + 1, 16):
        if t % cand == 0:
            best = cand
    assert best is not None, t
    return best


def _split_hi_lo(a):
    hi = a.astype(BF16)
    lo = (a - hi.astype(F32)).astype(BF16)
    return hi, lo


def _rmsnorm_rows(x, g):
    ms = jnp.mean(x * x, axis=-1, keepdims=True)
    return x * lax.rsqrt(ms + EPS) * g


def _softplus(z):
    return jnp.maximum(z, 0.0) + jnp.log(1.0 + jnp.exp(-jnp.abs(z)))


FFN_TILE = 256


def _resident(shape, layer=None):
    if layer is None:
        return pl.BlockSpec(shape, lambda *_: (0,) * len(shape), pipeline_mode=pl.Buffered(1))
    return pl.BlockSpec((None, *shape[1:]), lambda *_: (layer,) + (0,) * (len(shape) - 1),
                        pipeline_mode=pl.Buffered(1))


def _ffn_body(*refs, mixed):
    if mixed:
        h_ref, mo_ref, wo_ref, g_ref, wgu_ref, wd_ref, o_ref, xn_ref, acc_ref = refs
        o_ref[...] = h_ref[...] + jnp.dot(mo_ref[...], wo_ref[...], preferred_element_type=F32)
        base_ref = o_ref
    else:
        h_ref, g_ref, wgu_ref, wd_ref, o_ref, xn_ref, acc_ref = refs
        base_ref = h_ref
    dff = wd_ref.shape[0]
    tf = FFN_TILE
    nf = dff // tf
    xn_ref[...] = _rmsnorm_rows(base_ref[...], g_ref[...]).astype(BF16)
    for j in range(nf):
        xn = xn_ref[...]
        g = jnp.dot(xn, wgu_ref[:, j * tf:(j + 1) * tf], preferred_element_type=F32)
        u = jnp.dot(xn, wgu_ref[:, dff + j * tf:dff + (j + 1) * tf], preferred_element_type=F32)
        a = (g * jax.nn.sigmoid(g) * u).astype(BF16)
        part = jnp.dot(a, wd_ref[j * tf:(j + 1) * tf, :], preferred_element_type=F32)
        if j == 0:
            acc_ref[...] = part
        elif j < nf - 1:
            acc_ref[...] += part
        else:
            o_ref[...] = base_ref[...] + RES_HALF * (acc_ref[...] + part)


def _ffn(h, gain, w_gu, w_down, layer, mixer=None):
    t, d = h.shape
    assert w_down.shape[1] % FFN_TILE == 0
    tm = _row_tile(t)
    rows = pl.BlockSpec((tm, d), lambda i: (i, 0))
    operands, specs = [h], [rows]
    if mixer is not None:
        o, w_o, mixer_layer = mixer
        operands += [o, w_o]
        specs += [pl.BlockSpec((tm, o.shape[1]), lambda i: (i, 0)), _resident(w_o.shape, mixer_layer)]
    operands += [gain.reshape(1, d), w_gu, w_down]
    specs += [_resident((1, d)), _resident(w_gu.shape, layer), _resident(w_down.shape, layer)]
    return pl.pallas_call(
        functools.partial(_ffn_body, mixed=mixer is not None),
        out_shape=jax.ShapeDtypeStruct((t, d), F32),
        grid=(t // tm,),
        in_specs=specs,
        out_specs=rows,
        scratch_shapes=[pltpu.VMEM((tm, d), BF16), pltpu.VMEM((tm, d), F32)],
        compiler_params=pltpu.CompilerParams(
            dimension_semantics=("parallel",), vmem_limit_bytes=VMEM_LIMIT),
        name="ffn",
    )(*operands)


def _sb_proj_body(h_ref, g_ref, w_ref, hg_ref, seg_ref, o_ref):
    n = o_ref.shape[2]
    xn = _rmsnorm_rows(h_ref[...], g_ref[...]).astype(BF16)
    seg = seg_ref[...]
    for part in range(2):
        y = jnp.dot(xn, w_ref[:, part * n:(part + 1) * n], preferred_element_type=F32)
        for c in range(n // LANES):
            yc = y[:, c * LANES:(c + 1) * LANES]
            hi, lo = _split_hi_lo(yc * yc)
            ms = jnp.dot(jnp.concatenate([hi, lo], axis=1), seg, preferred_element_type=F32)
            gain = hg_ref[part, :, c * LANES:(c + 1) * LANES]
            o_ref[part, :, c * LANES:(c + 1) * LANES] = (yc * lax.rsqrt(ms + EPS) * gain).astype(BF16)
    o_ref[2] = jnp.dot(xn, w_ref[:, 2 * n:], preferred_element_type=F32).astype(BF16)


def _sb_proj(h, gain, w_qkv, layer, head_gains, seg):
    t, d = h.shape
    n = w_qkv.shape[2] // 3
    tm = _row_tile(t)
    return pl.pallas_call(
        _sb_proj_body,
        out_shape=jax.ShapeDtypeStruct((3, t, n), BF16),
        grid=(t // tm,),
        in_specs=[
            pl.BlockSpec((tm, d), lambda i: (i, 0)),
            _resident((1, d)),
            _resident(w_qkv.shape, layer),
            _resident(head_gains.shape),
            _resident(seg.shape),
        ],
        out_specs=pl.BlockSpec((3, tm, n), lambda i: (0, i, 0)),
        compiler_params=pltpu.CompilerParams(
            dimension_semantics=("parallel",), vmem_limit_bytes=VMEM_LIMIT),
        name="sb_proj",
    )(h, gain.reshape(1, d), w_qkv, head_gains, seg)


def _sb_attn_body(q_ref, k_ref, v_ref, cum_ref, o_ref, qp_ref, kp_ref, vp_ref, carry_ref, acc_ref,
                  flag_ref, *, seq, group):
    rows, tile = SB_ROWS, SB_TILE
    front = tile - rows
    nw = -(-seq // rows)
    lp = nw * rows
    ng = -(-nw // group)
    big = jnp.float32(3.0e38)

    qp_ref[0:seq, :] = q_ref[0]
    kp_ref[front:front + seq, :] = k_ref[0]
    vp_ref[front:front + seq, :] = v_ref[0]
    for dst, lo, hi in ((qp_ref, seq, lp), (kp_ref, 0, front), (vp_ref, 0, front),
                        (kp_ref, front + seq, front + lp), (vp_ref, front + seq, front + lp)):
        if hi > lo:
            dst[lo:hi, :] = jnp.zeros((hi - lo, LANES), dst.dtype)
    carry_ref[nw] = jnp.zeros(carry_ref.shape[1:], F32)
    acc_ref[nw] = jnp.zeros(acc_ref.shape[1:], F32)

    lane = lax.broadcasted_iota(jnp.int32, (1, LANES), 1)
    head0 = lane < SB_HEAD_DIM
    row = lax.broadcasted_iota(jnp.int32, (2 * rows, tile), 0)
    col = lax.broadcasted_iota(jnp.int32, (2 * rows, tile), 1)
    causal = col < jnp.where(row >= rows, row - rows, row) + front

    def group_step(g, d, first):
        zs, sps, vbs, slots, nexts = [], [], [], [], []
        for u in range(group):
            i = g * group + u
            i_c = jnp.minimum(i, nw - 1)
            valid = jnp.logical_and(i < nw, i * rows >= d * tile)
            slots.append(jnp.where(valid, i, nw))
            nexts.append(jnp.logical_and(i < nw, i * rows >= (d + 1) * tile))
            q = qp_ref[pl.ds(pl.multiple_of(i_c * rows, rows), rows), :]
            zero = jnp.zeros_like(q)
            q2 = jnp.concatenate([jnp.where(head0, q, zero), jnp.where(head0, zero, q)], axis=0)
            kt = pl.multiple_of(jnp.maximum(i_c * rows - d * tile, 0), rows)
            z = lax.dot_general(q2, kp_ref[pl.ds(kt, tile), :], _NT, preferred_element_type=F32)
            sp = _softplus(z)
            if first:
                sp = jnp.where(causal, sp, 0.0)
            zs.append(z)
            sps.append(sp.astype(BF16))
            vbs.append(vp_ref[pl.ds(kt, tile), :])
        cr = jnp.dot(jnp.concatenate(sps, axis=0), cum_ref[...], preferred_element_type=F32)
        low = big
        for u in range(group):
            cr_u = cr[u * 2 * rows:(u + 1) * 2 * rows]
            later, total = cr_u[:, :tile], cr_u[:, tile:]
            if not first:
                carry_old = carry_ref[slots[u]]
                later, total = later + carry_old, total + carry_old
            w = jnp.exp(zs[u] - later)
            if first:
                w = jnp.where(causal, w, 0.0)
            acc = jnp.dot(w.astype(BF16), vbs[u], preferred_element_type=F32)
            if not first:
                acc = acc + acc_ref[slots[u]]
            carry_ref[slots[u]] = total
            acc_ref[slots[u]] = acc
            low = jnp.minimum(low, jnp.where(nexts[u], jnp.min(total), big))
        flag_ref[g] = (low < SB_SKIP_LOG).astype(jnp.int32)

    def first_group(g, count):
        group_step(g, 0, True)
        return count + flag_ref[g]

    def later_step(state):
        d, _ = state

        def one_group(g, count):
            @pl.when(flag_ref[g] != 0)
            def _():
                group_step(g, d, False)
            return count + flag_ref[g]

        return d + 1, lax.fori_loop(0, ng, one_group, jnp.int32(0))

    pending = lax.fori_loop(0, ng, first_group, jnp.int32(0))
    lax.while_loop(lambda state: state[1] > 0, later_step, (jnp.int32(1), pending))

    def emit(i, start, n_rows):
        a = acc_ref[i]
        out = jnp.where(head0, a[:rows], a[rows:])
        o_ref[0, pl.ds(start, n_rows), :] = out[:n_rows].astype(o_ref.dtype)

    def emit_full(i, c):
        emit(i, pl.multiple_of(i * rows, rows), rows)
        return c

    lax.fori_loop(0, seq // rows, emit_full, 0)
    if lp > seq:
        emit(nw - 1, (nw - 1) * rows, seq - (nw - 1) * rows)


def _sb_attn(qkv, cum, batch, seq, *, group=11):
    n = qkv.shape[2]
    assert seq % 16 == 0 and n % LANES == 0 and SB_TILE % SB_ROWS == 0
    qkv = qkv.reshape(3, batch, seq, n)
    nw = -(-seq // SB_ROWS)
    ng = -(-nw // group)
    body = functools.partial(_sb_attn_body, seq=seq, group=group)

    def spec(which):
        return pl.BlockSpec((None, 1, seq, LANES), lambda b, p, which=which: (which, b, 0, p))

    q_pad = pltpu.VMEM((nw * SB_ROWS, LANES), BF16)
    kv_pad = pltpu.VMEM((SB_TILE - SB_ROWS + nw * SB_ROWS, LANES), BF16)
    state = pltpu.VMEM((nw + 1, 2 * SB_ROWS, SB_TILE), F32)
    out = pl.pallas_call(
        body,
        out_shape=jax.ShapeDtypeStruct((batch, seq, n), BF16),
        grid=(batch, n // LANES),
        in_specs=[spec(0), spec(1), spec(2),
                  pl.BlockSpec((SB_TILE, 2 * SB_TILE), lambda b, p: (0, 0))],
        out_specs=pl.BlockSpec((1, seq, LANES), lambda b, p: (b, 0, p)),
        scratch_shapes=[q_pad, kv_pad, kv_pad, state, state, pltpu.SMEM((ng,), jnp.int32)],
        compiler_params=pltpu.CompilerParams(
            dimension_semantics=("parallel", "parallel"), vmem_limit_bytes=VMEM_LIMIT),
        name="sb_attn",
    )(qkv, qkv, qkv, cum)
    return out.reshape(batch * seq, n)


def _sb_constants():
    r = jnp.arange(2 * LANES)[:, None] % LANES
    c = jnp.arange(LANES)[None, :]
    seg = jnp.where(r // SB_HEAD_DIM == c // SB_HEAD_DIM, 1.0 / SB_HEAD_DIM, 0.0).astype(BF16)
    j = jnp.arange(SB_TILE)[:, None]
    s = jnp.arange(2 * SB_TILE)[None, :]
    cum = jnp.where(jnp.logical_or(s >= SB_TILE, j >= s), 1.0, 0.0).astype(BF16)
    return seg, cum


def _sb_mixer(h, gain, w_qkv, g_q, g_k, layer, batch, seq):
    seg, cum = _sb_constants()
    scale = SB_HEAD_DIM ** -0.5
    head_gains = jnp.stack([jnp.tile(g_q, SB_HEADS) * scale, jnp.tile(g_k, SB_HEADS)])[:, None, :]
    qkv = _sb_proj(h, gain, w_qkv, layer, head_gains.astype(F32), seg)
    return _sb_attn(qkv, cum, batch, seq)


def _gla_proj_body(h_ref, g_ref, w_ref, wlow_ref, wup_ref, b_ref, o_ref, la_ref):
    xn = _rmsnorm_rows(h_ref[...], g_ref[...]).astype(BF16)
    g_low = jnp.dot(xn, wlow_ref[...], preferred_element_type=F32).astype(BF16)
    pre = jnp.dot(g_low, wup_ref[...], preferred_element_type=F32) + b_ref[...]
    la_ref[...] = (jnp.minimum(pre, 0.0) - jnp.log(1.0 + jnp.exp(-jnp.abs(pre)))) / GLA_TAU
    o_ref[...] = jnp.dot(xn, w_ref[:, :o_ref.shape[1]], preferred_element_type=F32)


def _gla_proj(h, gain, w_in, w_low, w_up, layer, b_gate):
    t, d = h.shape
    dk = w_up.shape[2]
    n = w_in.shape[2] - GLA_GATE_RANK
    tm = _row_tile(t)
    return pl.pallas_call(
        _gla_proj_body,
        out_shape=(jax.ShapeDtypeStruct((t, n), F32), jax.ShapeDtypeStruct((t, dk), F32)),
        grid=(t // tm,),
        in_specs=[
            pl.BlockSpec((tm, d), lambda i: (i, 0)),
            _resident((1, d)),
            _resident(w_in.shape, layer),
            _resident(w_low.shape, layer),
            _resident(w_up.shape, layer),
            _resident((1, dk)),
        ],
        out_specs=(pl.BlockSpec((tm, n), lambda i: (i, 0)),
                   pl.BlockSpec((tm, dk), lambda i: (i, 0))),
        compiler_params=pltpu.CompilerParams(
            dimension_semantics=("parallel",), vmem_limit_bytes=VMEM_LIMIT),
        name="gla_proj",
    )(h, gain.reshape(1, d), w_in, w_low, w_up, b_gate.reshape(1, dk))


def _gla_core_body(q_ref, k_ref, v_ref, r_ref, la_ref, gout_ref, tril_ref, o_ref,
                   st_ref, oi_ref, kv_ref, qd_ref, dec_ref, *, seq, hk, group):
    c = GLA_CHUNK
    lead = seq % c
    n_full = seq // c
    first = 1 if lead else 0
    scale = hk ** -0.5
    row = lax.broadcasted_iota(jnp.int32, (c, c), 0)
    col = lax.broadcasted_iota(jnp.int32, (c, c), 1)
    causal = col <= row

    def local(chunks):
        decay = []
        for _, _, _, _, a in chunks:
            hi, lo = _split_hi_lo(a)
            decay.append(jnp.dot(tril_ref[...], jnp.concatenate([hi, lo], axis=0),
                                 preferred_element_type=F32))
        atts, kss = [], []
        for (slot, q, k, _, _), b in zip(chunks, decay):
            b_last = b[c - 1:c, :]
            qd = (q * scale * jnp.exp(b)).astype(BF16)
            kd = (k * jnp.exp(-b)).astype(BF16)
            kss.append((k * jnp.exp(b_last - b)).astype(BF16))
            atts.append(lax.dot_general(qd, kd, _NT, preferred_element_type=F32))
            qd_ref[slot] = qd
            dec_ref[slot] = jnp.broadcast_to(jnp.exp(b_last), dec_ref.shape[1:])
        for (slot, _, _, v, _), att, ks in zip(chunks, atts, kss):
            att = jnp.where(causal, att, 0.0).astype(BF16)
            oi_ref[slot] = jnp.dot(att, v.astype(BF16), preferred_element_type=F32)
            kv_ref[slot] = jnp.dot(v.T.astype(BF16), ks, preferred_element_type=F32)

    def carried(slot, r):
        st = st_ref[...]
        o = oi_ref[slot] + lax.dot_general(qd_ref[slot], st.astype(BF16), _NT, preferred_element_type=F32)
        st_ref[...] = st * dec_ref[slot][0:1, :] + kv_ref[slot]
        o = o * lax.rsqrt(jnp.mean(o * o, axis=-1, keepdims=True) + EPS) * gout_ref[...]
        return (o * (r * jax.nn.sigmoid(r))).astype(o_ref.dtype)

    def padded(ref):
        x = ref[0, 0:lead, :]
        return jnp.concatenate([jnp.zeros((c - lead, x.shape[1]), x.dtype), x], axis=0)

    def rows_of(i):
        return pl.ds(pl.multiple_of(lead + i * c, 16), c)

    if lead:
        local([(0, padded(q_ref), padded(k_ref), padded(v_ref), padded(la_ref))])

    def local_group(gi, carry):
        chunks = []
        for u in range(group):
            i = gi * group + u
            rows = rows_of(i)
            chunks.append((first + i, q_ref[0, rows, :], k_ref[0, rows, :], v_ref[0, rows, :],
                           la_ref[0, rows, :]))
        local(chunks)
        return carry

    assert n_full % group == 0
    lax.fori_loop(0, n_full // group, local_group, 0)

    st_ref[...] = jnp.zeros_like(st_ref)
    if lead:
        o_ref[0, 0:lead, :] = carried(0, padded(r_ref))[c - lead:, :]

    def carried_step(i, carry):
        rows = rows_of(i)
        o_ref[0, rows, :] = carried(first + i, r_ref[0, rows, :])
        return carry

    lax.fori_loop(0, n_full, carried_step, 0, unroll=group)


def _gla_core(proj, la, g_out, batch, seq, dk, dv, *, group=8):
    hk, hv = dk // GLA_HEADS, dv // GLA_HEADS
    assert hk % LANES == 0 and hv % LANES == 0 and seq % 16 == 0 and (seq % GLA_CHUNK) % 16 == 0
    proj = proj.reshape(batch, seq, proj.shape[1])
    la = la.reshape(batch, seq, dk)
    tril = (jnp.arange(GLA_CHUNK)[:, None] >= jnp.arange(2 * GLA_CHUNK)[None, :] % GLA_CHUNK).astype(BF16)
    slots = -(-seq // GLA_CHUNK)
    group = group if (seq // GLA_CHUNK) % group == 0 else 1
    body = functools.partial(_gla_core_body, seq=seq, hk=hk, group=group)
    nk, nv = dk // hk, dv // hv
    out = pl.pallas_call(
        body,
        out_shape=jax.ShapeDtypeStruct((batch, seq, dv), BF16),
        grid=(batch, GLA_HEADS),
        in_specs=[
            pl.BlockSpec((1, seq, hk), lambda b, h: (b, 0, h)),
            pl.BlockSpec((1, seq, hk), lambda b, h: (b, 0, nk + h)),
            pl.BlockSpec((1, seq, hv), lambda b, h: (b, 0, (2 * dk) // hv + h)),
            pl.BlockSpec((1, seq, hv), lambda b, h: (b, 0, (2 * dk) // hv + nv + h)),
            pl.BlockSpec((1, seq, hk), lambda b, h: (b, 0, h)),
            pl.BlockSpec((1, hv), lambda b, h: (0, h)),
            pl.BlockSpec((GLA_CHUNK, 2 * GLA_CHUNK), lambda b, h: (0, 0)),
        ],
        out_specs=pl.BlockSpec((1, seq, hv), lambda b, h: (b, 0, h)),
        scratch_shapes=[
            pltpu.VMEM((hv, hk), F32),
            pltpu.VMEM((slots, GLA_CHUNK, hv), F32),
            pltpu.VMEM((slots, hv, hk), F32),
            pltpu.VMEM((slots, GLA_CHUNK, hk), BF16),
            pltpu.VMEM((slots, 8, hk), F32),
        ],
        compiler_params=pltpu.CompilerParams(
            dimension_semantics=("parallel", "parallel"), vmem_limit_bytes=VMEM_LIMIT),
        name="gla_core",
    )(proj, proj, proj, proj, la, g_out.reshape(1, dv), tril)
    return out.reshape(batch * seq, dv)


def _gla_mixer(h, gain, w_in, w_low, w_up, b_gate, g_out, layer, batch, seq):
    dk = w_up.shape[2]
    dv = g_out.shape[0]
    proj, la = _gla_proj(h, gain, w_in, w_low, w_up, layer, b_gate)
    return _gla_core(proj, la, g_out, batch, seq, dk, dv)


def kernel(x, meta, ffn_a_norm, ffn_a_w_gu, ffn_a_w_down, mix_norm, sb_w_qkv, sb_q_norm, sb_k_norm, sb_w_o, gla_w_in, gla_w_gate_up, gla_b_gate, gla_out_norm, gla_w_o, ffn_b_norm, ffn_b_w_gu, ffn_b_w_down):
    batch, _, d = x.shape
    depth = ffn_a_norm.shape[0]
    m = jnp.broadcast_to(meta.astype(x.dtype)[None], (batch, meta.shape[0], d))
    h = jnp.concatenate([m, x], axis=1)
    seq = h.shape[1]
    h = h.reshape(batch * seq, d)
    bf = lambda w: w.astype(BF16)
    a_gu, a_down, b_gu, b_down = bf(ffn_a_w_gu), bf(ffn_a_w_down), bf(ffn_b_w_gu), bf(ffn_b_w_down)
    w_qkv, w_sbo, w_in, w_glao = bf(sb_w_qkv), bf(sb_w_o), bf(gla_w_in), bf(gla_w_o)
    low_pad = LANES - GLA_GATE_RANK
    w_low = jnp.pad(w_in[:, :, w_in.shape[2] - GLA_GATE_RANK:], ((0, 0), (0, 0), (0, low_pad)))
    w_up = jnp.pad(bf(gla_w_gate_up), ((0, 0), (0, low_pad), (0, 0)))
    for i in range(depth):
        h = _ffn(h, ffn_a_norm[i], a_gu, a_down, i)
        j = i // 2
        if i % 2 == 0:
            o = _sb_mixer(h, mix_norm[i], w_qkv, sb_q_norm[j], sb_k_norm[j], j, batch, seq)
            w_o = w_sbo
        else:
            o = _gla_mixer(h, mix_norm[i], w_in, w_low, w_up, gla_b_gate[j], gla_out_norm[j], j, batch, seq)
            w_o = w_glao
        h = _ffn(h, ffn_b_norm[i], b_gu, b_down, i, mixer=(o, w_o, j))
    return h.reshape(batch, seq, d)[:, meta.shape[0]:]
```

```python
import functools

import jax
import jax.numpy as jnp
from jax import lax
from jax.experimental import pallas as pl
from jax.experimental.pallas import tpu as pltpu

F32 = jnp.float32
BF16 = jnp.bfloat16

N_META = 16
RES_HALF = 0.5
EPS = 1e-6
SB_HEADS = 16
SB_HEAD_DIM = 64
GLA_HEADS = 4
GLA_GATE_RANK = 16
GLA_TAU = 16.0
GLA_CHUNK = 64

LANES = 128
SB_ROWS = 64
SB_TILE = 128
SEG_WIDTH = 256
SB_SKIP_LOG = 104.0
VMEM_LIMIT = 56 * 1024 * 1024

_NT = (((1,), (1,)), ((), ()))


def _row_tile(t, limit=1024):
    best = None
    for cand in range(16, min(t, limit) + 1, 16):
        if t % cand == 0:
            best = cand
    assert best is not None, t
    return best


def _split_hi_lo(a):
    hi = a.astype(BF16)
    lo = (a - hi.astype(F32)).astype(BF16)
    return hi, lo


def _rmsnorm_rows(x, g):
    ms = jnp.mean(x * x, axis=-1, keepdims=True)
    return x * lax.rsqrt(ms + EPS) * g


def _softplus(z):
    return jnp.maximum(z, 0.0) + jnp.log(1.0 + jnp.exp(-jnp.abs(z)))


FFN_TILE = 256


def _resident(shape, layer=None):
    if layer is None:
        return pl.BlockSpec(shape, lambda *_: (0,) * len(shape), pipeline_mode=pl.Buffered(1))
    return pl.BlockSpec((None, *shape[1:]), lambda *_: (layer,) + (0,) * (len(shape) - 1),
                        pipeline_mode=pl.Buffered(1))


def _ffn_rows(base_ref, g_ref, wgu_ref, wd_ref, o_ref, xn_ref, acc_ref):
    dff = wd_ref.shape[0]
    tf = FFN_TILE
    nf = dff // tf
    xn_ref[...] = _rmsnorm_rows(base_ref[...], g_ref[...]).astype(BF16)
    for j in range(nf):
        xn = xn_ref[...]
        g = jnp.dot(xn, wgu_ref[:, j * tf:(j + 1) * tf], preferred_element_type=F32)
        u = jnp.dot(xn, wgu_ref[:, dff + j * tf:dff + (j + 1) * tf], preferred_element_type=F32)
        a = (g * jax.nn.sigmoid(g) * u).astype(BF16)
        part = jnp.dot(a, wd_ref[j * tf:(j + 1) * tf, :], preferred_element_type=F32)
        if j == 0:
            acc_ref[...] = part
        elif j < nf - 1:
            acc_ref[...] += part
        else:
            o_ref[...] = base_ref[...] + RES_HALF * (acc_ref[...] + part)


def _ffn_body(h_ref, *refs):
    _ffn_rows(h_ref, *refs)


def _ffn_mixed_body(h_ref, mo_ref, wo_ref, g_ref, wgu_ref, wd_ref, o_ref, xn_ref, acc_ref):
    o_ref[...] = h_ref[...] + jnp.dot(mo_ref[...], wo_ref[...], preferred_element_type=F32)
    _ffn_rows(o_ref, g_ref, wgu_ref, wd_ref, o_ref, xn_ref, acc_ref)


def _ffn_last_body(h_ref, mo_ref, wo_ref, g_ref, wgu_ref, wd_ref, o_ref, xn_ref, acc_ref):
    o_ref[...] = h_ref[0] + jnp.dot(mo_ref[0], wo_ref[...], preferred_element_type=F32)
    _ffn_rows(o_ref, g_ref, wgu_ref, wd_ref, o_ref, xn_ref, acc_ref)


def _ffn_first_body(x_ref, meta_ref, g_ref, wgu_ref, wd_ref, o_ref, xn_ref, acc_ref):
    n_meta = meta_ref.shape[0]
    xt = x_ref[0]
    lead = jnp.concatenate([meta_ref[...], xt[:xt.shape[0] - n_meta]], axis=0)
    o_ref[...] = jnp.where(pl.program_id(1) == 0, lead, xt)
    _ffn_rows(o_ref, g_ref, wgu_ref, wd_ref, o_ref, xn_ref, acc_ref)


def _ffn_call(body, grid, row_operands, row_specs, gain, w_gu, w_down, layer, out_shape, out_spec, tm):
    d = gain.shape[0]
    assert w_down.shape[1] % FFN_TILE == 0
    return pl.pallas_call(
        body,
        out_shape=out_shape,
        grid=grid,
        in_specs=[*row_specs, _resident((1, d)), _resident(w_gu.shape, layer), _resident(w_down.shape, layer)],
        out_specs=out_spec,
        scratch_shapes=[pltpu.VMEM((tm, d), BF16), pltpu.VMEM((tm, d), F32)],
        compiler_params=pltpu.CompilerParams(
            dimension_semantics=("parallel",) * len(grid), vmem_limit_bytes=VMEM_LIMIT),
        name="ffn",
    )(*row_operands, gain.reshape(1, d), w_gu, w_down)


def _ffn(h, gain, w_gu, w_down, layer, mixer=None):
    t, d = h.shape
    tm = _row_tile(t)
    rows = pl.BlockSpec((tm, d), lambda i: (i, 0))
    if mixer is None:
        body, operands, specs = _ffn_body, [h], [rows]
    else:
        o, w_o, mixer_layer = mixer
        body, operands = _ffn_mixed_body, [h, o, w_o]
        specs = [rows, pl.BlockSpec((tm, o.shape[1]), lambda i: (i, 0)), _resident(w_o.shape, mixer_layer)]
    return _ffn_call(body, (t // tm,), operands, specs, gain, w_gu, w_down, layer,
                     jax.ShapeDtypeStruct((t, d), F32), rows, tm)


def _ffn_first(x, meta, gain, w_gu, w_down, layer):
    batch, s, d = x.shape
    n_meta = meta.shape[0]
    seq = n_meta + s
    tm = _row_tile(seq)
    nt = seq // tm
    assert n_meta % 8 == 0 and n_meta < tm
    window = pl.BlockSpec((pl.Element(1), pl.Element(tm), pl.Element(d)),
                          lambda b, t: (b, pl.multiple_of(jnp.maximum(t * tm - n_meta, 0), 8), 0))
    return _ffn_call(_ffn_first_body, (batch, nt), [x, meta], [window, _resident(meta.shape)],
                     gain, w_gu, w_down, layer, jax.ShapeDtypeStruct((batch * seq, d), F32),
                     pl.BlockSpec((tm, d), lambda b, t: (b * nt + t, 0)), tm)


def _ffn_last(h, gain, w_gu, w_down, layer, mixer, batch, n_meta):
    t, d = h.shape
    seq = t // batch
    s = seq - n_meta
    tm = _row_tile(s, 512)
    o, w_o, mixer_layer = mixer
    k = o.shape[1]

    def window(width):
        return pl.BlockSpec((pl.Element(1), pl.Element(tm), pl.Element(width)),
                            lambda b, i: (b, pl.multiple_of(n_meta + i * tm, 8), 0))

    return _ffn_call(_ffn_last_body, (batch, s // tm),
                     [h.reshape(batch, seq, d), o.reshape(batch, seq, k), w_o],
                     [window(d), window(k), _resident(w_o.shape, mixer_layer)],
                     gain, w_gu, w_down, layer, jax.ShapeDtypeStruct((batch, s, d), F32),
                     pl.BlockSpec((None, tm, d), lambda b, i: (b, i, 0)), tm)


def _sb_proj_body(h_ref, g_ref, w_ref, hg_ref, seg_ref, o_ref):
    n = o_ref.shape[2]
    xn = _rmsnorm_rows(h_ref[...], g_ref[...]).astype(BF16)
    seg = seg_ref[...]
    width = seg_ref.shape[0]
    for part in range(2):
        y = jnp.dot(xn, w_ref[:, part * n:(part + 1) * n], preferred_element_type=F32)
        for c in range(n // width):
            cols = slice(c * width, (c + 1) * width)
            yc = y[:, cols]
            ms = jnp.dot((yc * yc).astype(BF16), seg, preferred_element_type=F32)
            o_ref[part, :, cols] = (yc * lax.rsqrt(ms + EPS) * hg_ref[part, :, cols]).astype(BF16)
    o_ref[2] = jnp.dot(xn, w_ref[:, 2 * n:], preferred_element_type=F32).astype(BF16)


def _sb_proj(h, gain, w_qkv, layer, head_gains, seg):
    t, d = h.shape
    n = w_qkv.shape[2] // 3
    tm = _row_tile(t)
    return pl.pallas_call(
        _sb_proj_body,
        out_shape=jax.ShapeDtypeStruct((3, t, n), BF16),
        grid=(t // tm,),
        in_specs=[
            pl.BlockSpec((tm, d), lambda i: (i, 0)),
            _resident((1, d)),
            _resident(w_qkv.shape, layer),
            _resident(head_gains.shape),
            _resident(seg.shape),
        ],
        out_specs=pl.BlockSpec((3, tm, n), lambda i: (0, i, 0)),
        compiler_params=pltpu.CompilerParams(
            dimension_semantics=("parallel",), vmem_limit_bytes=VMEM_LIMIT),
        name="sb_proj",
    )(h, gain.reshape(1, d), w_qkv, head_gains, seg)


def _sb_attn_body(q_ref, k_ref, v_ref, cum_ref, o_ref, qp_ref, kp_ref, vp_ref, carry_ref, acc_ref,
                  flag_ref, *, seq, group):
    rows, tile = SB_ROWS, SB_TILE
    front = tile - rows
    nw = -(-seq // rows)
    lp = nw * rows
    ng = -(-nw // group)
    big = jnp.float32(3.0e38)

    qp_ref[0:seq, :] = q_ref[0]
    kp_ref[front:front + seq, :] = k_ref[0]
    vp_ref[front:front + seq, :] = v_ref[0]
    for dst, lo, hi in ((qp_ref, seq, lp), (kp_ref, 0, front), (vp_ref, 0, front),
                        (kp_ref, front + seq, front + lp), (vp_ref, front + seq, front + lp)):
        if hi > lo:
            dst[lo:hi, :] = jnp.zeros((hi - lo, LANES), dst.dtype)
    carry_ref[nw] = jnp.zeros(carry_ref.shape[1:], F32)
    acc_ref[nw] = jnp.zeros(acc_ref.shape[1:], F32)

    lane = lax.broadcasted_iota(jnp.int32, (1, LANES), 1)
    head0 = lane < SB_HEAD_DIM
    row = lax.broadcasted_iota(jnp.int32, (2 * rows, tile), 0)
    col = lax.broadcasted_iota(jnp.int32, (2 * rows, tile), 1)
    causal = col < jnp.where(row >= rows, row - rows, row) + front

    def group_step(g, d, first):
        zs, sps, vbs, slots, nexts = [], [], [], [], []
        for u in range(group):
            i = g * group + u
            i_c = jnp.minimum(i, nw - 1)
            valid = jnp.logical_and(i < nw, i * rows >= d * tile)
            slots.append(jnp.where(valid, i, nw))
            nexts.append(jnp.logical_and(i < nw, i * rows >= (d + 1) * tile))
            q = qp_ref[pl.ds(pl.multiple_of(i_c * rows, rows), rows), :]
            zero = jnp.zeros_like(q)
            q2 = jnp.concatenate([jnp.where(head0, q, zero), jnp.where(head0, zero, q)], axis=0)
            kt = pl.multiple_of(jnp.maximum(i_c * rows - d * tile, 0), rows)
            z = lax.dot_general(q2, kp_ref[pl.ds(kt, tile), :], _NT, preferred_element_type=F32)
            sp = _softplus(z)
            if first:
                sp = jnp.where(causal, sp, 0.0)
            zs.append(z)
            sps.append(sp.astype(BF16))
            vbs.append(vp_ref[pl.ds(kt, tile), :])
        cr = jnp.dot(jnp.concatenate(sps, axis=0), cum_ref[...], preferred_element_type=F32)
        low = big
        for u in range(group):
            cr_u = cr[u * 2 * rows:(u + 1) * 2 * rows]
            later, total = cr_u[:, :tile], cr_u[:, tile:]
            if not first:
                carry_old = carry_ref[slots[u]]
                later, total = later + carry_old, total + carry_old
            w = jnp.exp(zs[u] - later)
            if first:
                w = jnp.where(causal, w, 0.0)
            acc = jnp.dot(w.astype(BF16), vbs[u], preferred_element_type=F32)
            if not first:
                acc = acc + acc_ref[slots[u]]
            carry_ref[slots[u]] = total
            acc_ref[slots[u]] = acc
            low = jnp.minimum(low, jnp.where(nexts[u], jnp.min(total), big))
        flag_ref[g] = (low < SB_SKIP_LOG).astype(jnp.int32)

    def first_group(g, count):
        group_step(g, 0, True)
        return count + flag_ref[g]

    def later_step(state):
        d, _ = state

        def one_group(g, count):
            @pl.when(flag_ref[g] != 0)
            def _():
                group_step(g, d, False)
            return count + flag_ref[g]

        return d + 1, lax.fori_loop(0, ng, one_group, jnp.int32(0))

    pending = lax.fori_loop(0, ng, first_group, jnp.int32(0))
    lax.while_loop(lambda state: state[1] > 0, later_step, (jnp.int32(1), pending))

    def emit(i, start, n_rows):
        a = acc_ref[i]
        out = jnp.where(head0, a[:rows], a[rows:])
        o_ref[0, pl.ds(start, n_rows), :] = out[:n_rows].astype(o_ref.dtype)

    def emit_full(i, c):
        emit(i, pl.multiple_of(i * rows, rows), rows)
        return c

    lax.fori_loop(0, seq // rows, emit_full, 0)
    if lp > seq:
        emit(nw - 1, (nw - 1) * rows, seq - (nw - 1) * rows)


def _sb_attn(qkv, cum, batch, seq, *, group=33):
    n = qkv.shape[2]
    assert seq % 16 == 0 and n % LANES == 0 and SB_TILE % SB_ROWS == 0
    qkv = qkv.reshape(3, batch, seq, n)
    nw = -(-seq // SB_ROWS)
    ng = -(-nw // group)
    body = functools.partial(_sb_attn_body, seq=seq, group=group)

    def spec(which):
        return pl.BlockSpec((None, 1, seq, LANES), lambda b, p, which=which: (which, b, 0, p))

    q_pad = pltpu.VMEM((nw * SB_ROWS, LANES), BF16)
    kv_pad = pltpu.VMEM((SB_TILE - SB_ROWS + nw * SB_ROWS, LANES), BF16)
    state = pltpu.VMEM((nw + 1, 2 * SB_ROWS, SB_TILE), F32)
    out = pl.pallas_call(
        body,
        out_shape=jax.ShapeDtypeStruct((batch, seq, n), BF16),
        grid=(batch, n // LANES),
        in_specs=[spec(0), spec(1), spec(2),
                  pl.BlockSpec((SB_TILE, 2 * SB_TILE), lambda b, p: (0, 0))],
        out_specs=pl.BlockSpec((1, seq, LANES), lambda b, p: (b, 0, p)),
        scratch_shapes=[q_pad, kv_pad, kv_pad, state, state, pltpu.SMEM((ng,), jnp.int32)],
        compiler_params=pltpu.CompilerParams(
            dimension_semantics=("parallel", "parallel"), vmem_limit_bytes=VMEM_LIMIT),
        name="sb_attn",
    )(qkv, qkv, qkv, cum)
    return out.reshape(batch * seq, n)


def _sb_constants():
    r = jnp.arange(SEG_WIDTH)[:, None]
    c = jnp.arange(SEG_WIDTH)[None, :]
    seg = jnp.where(r // SB_HEAD_DIM == c // SB_HEAD_DIM, 1.0 / SB_HEAD_DIM, 0.0).astype(BF16)
    j = jnp.arange(SB_TILE)[:, None]
    s = jnp.arange(2 * SB_TILE)[None, :]
    cum = jnp.where(jnp.logical_or(s >= SB_TILE, j >= s), 1.0, 0.0).astype(BF16)
    return seg, cum


def _sb_mixer(h, gain, w_qkv, g_q, g_k, layer, batch, seq):
    seg, cum = _sb_constants()
    scale = SB_HEAD_DIM ** -0.5
    head_gains = jnp.stack([jnp.tile(g_q, SB_HEADS) * scale, jnp.tile(g_k, SB_HEADS)])[:, None, :]
    qkv = _sb_proj(h, gain, w_qkv, layer, head_gains.astype(F32), seg)
    return _sb_attn(qkv, cum, batch, seq)


def _gla_proj_body(h_ref, g_ref, w_ref, wlow_ref, wup_ref, b_ref, o_ref, la_ref):
    xn = _rmsnorm_rows(h_ref[...], g_ref[...]).astype(BF16)
    g_low = jnp.dot(xn, wlow_ref[...], preferred_element_type=F32).astype(BF16)
    pre = jnp.dot(g_low, wup_ref[...], preferred_element_type=F32) + b_ref[...]
    la_ref[...] = (jnp.minimum(pre, 0.0) - jnp.log(1.0 + jnp.exp(-jnp.abs(pre)))) / GLA_TAU
    o_ref[...] = jnp.dot(xn, w_ref[:, :o_ref.shape[1]], preferred_element_type=F32)


def _gla_proj(h, gain, w_in, w_low, w_up, layer, b_gate):
    t, d = h.shape
    dk = w_up.shape[2]
    n = w_in.shape[2] - GLA_GATE_RANK
    tm = _row_tile(t)
    return pl.pallas_call(
        _gla_proj_body,
        out_shape=(jax.ShapeDtypeStruct((t, n), F32), jax.ShapeDtypeStruct((t, dk), F32)),
        grid=(t // tm,),
        in_specs=[
            pl.BlockSpec((tm, d), lambda i: (i, 0)),
            _resident((1, d)),
            _resident(w_in.shape, layer),
            _resident(w_low.shape, layer),
            _resident(w_up.shape, layer),
            _resident((1, dk)),
        ],
        out_specs=(pl.BlockSpec((tm, n), lambda i: (i, 0)),
                   pl.BlockSpec((tm, dk), lambda i: (i, 0))),
        compiler_params=pltpu.CompilerParams(
            dimension_semantics=("parallel",), vmem_limit_bytes=VMEM_LIMIT),
        name="gla_proj",
    )(h, gain.reshape(1, d), w_in, w_low, w_up, b_gate.reshape(1, dk))


def _gla_core_body(q_ref, k_ref, v_ref, r_ref, la_ref, gout_ref, tril_ref, o_ref,
                   st_ref, oi_ref, kv_ref, qd_ref, dec_ref, *, seq, hk, group):
    c = GLA_CHUNK
    lead = seq % c
    n_full = seq // c
    first = 1 if lead else 0
    scale = hk ** -0.5
    row = lax.broadcasted_iota(jnp.int32, (c, c), 0)
    col = lax.broadcasted_iota(jnp.int32, (c, c), 1)
    causal = col <= row

    def local(chunks):
        decay = []
        for _, _, _, _, a in chunks:
            hi, lo = _split_hi_lo(a)
            decay.append(jnp.dot(tril_ref[...], jnp.concatenate([hi, lo], axis=0),
                                 preferred_element_type=F32))
        atts, kss = [], []
        for (slot, q, k, _, _), b in zip(chunks, decay):
            b_last = b[c - 1:c, :]
            qd = (q * scale * jnp.exp(b)).astype(BF16)
            kd = (k * jnp.exp(-b)).astype(BF16)
            kss.append((k * jnp.exp(b_last - b)).astype(BF16))
            atts.append(lax.dot_general(qd, kd, _NT, preferred_element_type=F32))
            qd_ref[slot] = qd
            dec_ref[slot] = jnp.broadcast_to(jnp.exp(b_last), dec_ref.shape[1:])
        for (slot, _, _, v, _), att, ks in zip(chunks, atts, kss):
            att = jnp.where(causal, att, 0.0).astype(BF16)
            oi_ref[slot] = jnp.dot(att, v.astype(BF16), preferred_element_type=F32)
            kv_ref[slot] = jnp.dot(v.T.astype(BF16), ks, preferred_element_type=F32)

    def carried(slot, r):
        st = st_ref[...]
        o = oi_ref[slot] + lax.dot_general(qd_ref[slot], st.astype(BF16), _NT, preferred_element_type=F32)
        st_ref[...] = st * dec_ref[slot][0:1, :] + kv_ref[slot]
        o = o * lax.rsqrt(jnp.mean(o * o, axis=-1, keepdims=True) + EPS) * gout_ref[...]
        return (o * (r * jax.nn.sigmoid(r))).astype(o_ref.dtype)

    def padded(ref):
        x = ref[0, 0:lead, :]
        return jnp.concatenate([jnp.zeros((c - lead, x.shape[1]), x.dtype), x], axis=0)

    def rows_of(i):
        return pl.ds(pl.multiple_of(lead + i * c, 16), c)

    def chunk_inputs(i):
        rows = rows_of(i)
        return (first + i, q_ref[0, rows, :], k_ref[0, rows, :], v_ref[0, rows, :], la_ref[0, rows, :])

    def local_group(gi, carry):
        local([chunk_inputs(gi * group + u) for u in range(group)])
        return carry

    def carried_step(i, carry):
        rows = rows_of(i)
        o_ref[0, rows, :] = carried(first + i, r_ref[0, rows, :])
        return carry

    assert n_full % group == 0
    n_groups = n_full // group
    head = [(0, padded(q_ref), padded(k_ref), padded(v_ref), padded(la_ref))] if lead else []
    local(head + [chunk_inputs(u) for u in range(group)])
    lax.fori_loop(1, n_groups, local_group, 0)

    st_ref[...] = jnp.zeros_like(st_ref)
    if lead:
        o_ref[0, 0:lead, :] = carried(0, padded(r_ref))[c - lead:, :]
    for u in range(group):
        carried_step(u, 0)
    lax.fori_loop(group, n_full, carried_step, 0, unroll=group)


def _gla_core(proj, la, g_out, batch, seq, dk, dv, *, group=8):
    hk, hv = dk // GLA_HEADS, dv // GLA_HEADS
    assert hk % LANES == 0 and hv % LANES == 0 and seq % 16 == 0 and (seq % GLA_CHUNK) % 16 == 0
    proj = proj.reshape(batch, seq, proj.shape[1])
    la = la.reshape(batch, seq, dk)
    tril = (jnp.arange(GLA_CHUNK)[:, None] >= jnp.arange(2 * GLA_CHUNK)[None, :] % GLA_CHUNK).astype(BF16)
    slots = -(-seq // GLA_CHUNK)
    group = group if (seq // GLA_CHUNK) % group == 0 else 1
    body = functools.partial(_gla_core_body, seq=seq, hk=hk, group=group)
    nk, nv = dk // hk, dv // hv
    out = pl.pallas_call(
        body,
        out_shape=jax.ShapeDtypeStruct((batch, seq, dv), BF16),
        grid=(batch, GLA_HEADS),
        in_specs=[
            pl.BlockSpec((1, seq, hk), lambda b, h: (b, 0, h)),
            pl.BlockSpec((1, seq, hk), lambda b, h: (b, 0, nk + h)),
            pl.BlockSpec((1, seq, hv), lambda b, h: (b, 0, (2 * dk) // hv + h)),
            pl.BlockSpec((1, seq, hv), lambda b, h: (b, 0, (2 * dk) // hv + nv + h)),
            pl.BlockSpec((1, seq, hk), lambda b, h: (b, 0, h)),
            pl.BlockSpec((1, hv), lambda b, h: (0, h)),
            pl.BlockSpec((GLA_CHUNK, 2 * GLA_CHUNK), lambda b, h: (0, 0)),
        ],
        out_specs=pl.BlockSpec((1, seq, hv), lambda b, h: (b, 0, h)),
        scratch_shapes=[
            pltpu.VMEM((hv, hk), F32),
            pltpu.VMEM((slots, GLA_CHUNK, hv), F32),
            pltpu.VMEM((slots, hv, hk), F32),
            pltpu.VMEM((slots, GLA_CHUNK, hk), BF16),
            pltpu.VMEM((slots, 8, hk), F32),
        ],
        compiler_params=pltpu.CompilerParams(
            dimension_semantics=("parallel", "parallel"), vmem_limit_bytes=VMEM_LIMIT),
        name="gla_core",
    )(proj, proj, proj, proj, la, g_out.reshape(1, dv), tril)
    return out.reshape(batch * seq, dv)


def _gla_mixer(h, gain, w_in, w_low, w_up, b_gate, g_out, layer, batch, seq):
    dk = w_up.shape[2]
    dv = g_out.shape[0]
    proj, la = _gla_proj(h, gain, w_in, w_low, w_up, layer, b_gate)
    return _gla_core(proj, la, g_out, batch, seq, dk, dv)


def kernel(x, meta, ffn_a_norm, ffn_a_w_gu, ffn_a_w_down, mix_norm, sb_w_qkv, sb_q_norm, sb_k_norm, sb_w_o, gla_w_in, gla_w_gate_up, gla_b_gate, gla_out_norm, gla_w_o, ffn_b_norm, ffn_b_w_gu, ffn_b_w_down):
    batch, s, d = x.shape
    depth = ffn_a_norm.shape[0]
    n_meta = meta.shape[0]
    seq = n_meta + s
    bf = lambda w: w.astype(BF16)
    a_gu, a_down, b_gu, b_down = bf(ffn_a_w_gu), bf(ffn_a_w_down), bf(ffn_b_w_gu), bf(ffn_b_w_down)
    w_qkv, w_sbo, w_in, w_glao = bf(sb_w_qkv), bf(sb_w_o), bf(gla_w_in), bf(gla_w_o)
    low_pad = LANES - GLA_GATE_RANK
    w_low = jnp.pad(w_in[:, :, w_in.shape[2] - GLA_GATE_RANK:], ((0, 0), (0, 0), (0, low_pad)))
    w_up = jnp.pad(bf(gla_w_gate_up), ((0, 0), (0, low_pad), (0, 0)))
    h = None
    for i in range(depth):
        if i == 0:
            h = _ffn_first(x, meta.astype(x.dtype), ffn_a_norm[i], a_gu, a_down, i)
        else:
            h = _ffn(h, ffn_a_norm[i], a_gu, a_down, i)
        j = i // 2
        if i % 2 == 0:
            o = _sb_mixer(h, mix_norm[i], w_qkv, sb_q_norm[j], sb_k_norm[j], j, batch, seq)
            w_o = w_sbo
        else:
            o = _gla_mixer(h, mix_norm[i], w_in, w_low, w_up, gla_b_gate[j], gla_out_norm[j], j, batch, seq)
            w_o = w_glao
        if i < depth - 1:
            h = _ffn(h, ffn_b_norm[i], b_gu, b_down, i, mixer=(o, w_o, j))
        else:
            h = _ffn_last(h, ffn_b_norm[i], b_gu, b_down, i, (o, w_o, j), batch, n_meta)
    return h
```

```python
import functools

import jax
import jax.numpy as jnp
from jax import lax
from jax.experimental import pallas as pl
from jax.experimental.pallas import tpu as pltpu

F32 = jnp.float32
BF16 = jnp.bfloat16

N_META = 16
RES_HALF = 0.5
EPS = 1e-6
SB_HEADS = 16
SB_HEAD_DIM = 64
GLA_HEADS = 4
GLA_GATE_RANK = 16
GLA_TAU = 16.0
GLA_CHUNK = 64

LANES = 128
SB_ROWS = 64
SB_TILE = 256
SEG_WIDTH = 256
SB_SKIP_LOG = 104.0
VMEM_LIMIT = 56 * 1024 * 1024

_NT = (((1,), (1,)), ((), ()))


def _row_tile(t, limit=1024):
    best = None
    for cand in range(16, min(t, limit) + 1, 16):
        if t % cand == 0:
            best = cand
    assert best is not None, t
    return best


def _split_hi_lo(a):
    hi = a.astype(BF16)
    lo = (a - hi.astype(F32)).astype(BF16)
    return hi, lo


def _rmsnorm_rows(x, g):
    ms = jnp.mean(x * x, axis=-1, keepdims=True)
    return x * lax.rsqrt(ms + EPS) * g


def _softplus(z):
    return jnp.maximum(z, 0.0) + jnp.log(1.0 + jnp.exp(-jnp.abs(z)))


FFN_TILE = 256


def _resident(shape, layer=None):
    if layer is None:
        return pl.BlockSpec(shape, lambda *_: (0,) * len(shape), pipeline_mode=pl.Buffered(1))
    return pl.BlockSpec((None, *shape[1:]), lambda *_: (layer,) + (0,) * (len(shape) - 1),
                        pipeline_mode=pl.Buffered(1))


def _ffn_rows(base_ref, g_ref, wgu_ref, wd_ref, o_ref, xn_ref, acc_ref):
    dff = wd_ref.shape[0]
    tf = FFN_TILE
    nf = dff // tf
    xn_ref[...] = _rmsnorm_rows(base_ref[...], g_ref[...]).astype(BF16)
    for j in range(nf):
        xn = xn_ref[...]
        g = jnp.dot(xn, wgu_ref[:, j * tf:(j + 1) * tf], preferred_element_type=F32)
        u = jnp.dot(xn, wgu_ref[:, dff + j * tf:dff + (j + 1) * tf], preferred_element_type=F32)
        a = (g * jax.nn.sigmoid(g) * u).astype(BF16)
        part = jnp.dot(a, wd_ref[j * tf:(j + 1) * tf, :], preferred_element_type=F32)
        if j == 0:
            acc_ref[...] = part
        elif j < nf - 1:
            acc_ref[...] += part
        else:
            o_ref[...] = base_ref[...] + RES_HALF * (acc_ref[...] + part)


def _ffn_body(h_ref, *refs):
    _ffn_rows(h_ref, *refs)


def _ffn_mixed_body(h_ref, mo_ref, wo_ref, g_ref, wgu_ref, wd_ref, o_ref, xn_ref, acc_ref):
    o_ref[...] = h_ref[...] + jnp.dot(mo_ref[...], wo_ref[...], preferred_element_type=F32)
    _ffn_rows(o_ref, g_ref, wgu_ref, wd_ref, o_ref, xn_ref, acc_ref)


def _ffn_last_body(h_ref, mo_ref, wo_ref, g_ref, wgu_ref, wd_ref, o_ref, xn_ref, acc_ref):
    o_ref[...] = h_ref[0] + jnp.dot(mo_ref[0], wo_ref[...], preferred_element_type=F32)
    _ffn_rows(o_ref, g_ref, wgu_ref, wd_ref, o_ref, xn_ref, acc_ref)


def _ffn_first_body(x_ref, meta_ref, g_ref, wgu_ref, wd_ref, o_ref, xn_ref, acc_ref):
    n_meta = meta_ref.shape[0]
    xt = x_ref[0]
    lead = jnp.concatenate([meta_ref[...], xt[:xt.shape[0] - n_meta]], axis=0)
    o_ref[...] = jnp.where(pl.program_id(1) == 0, lead, xt)
    _ffn_rows(o_ref, g_ref, wgu_ref, wd_ref, o_ref, xn_ref, acc_ref)


def _ffn_call(body, grid, row_operands, row_specs, gain, w_gu, w_down, layer, out_shape, out_spec, tm):
    d = gain.shape[0]
    assert w_down.shape[1] % FFN_TILE == 0
    return pl.pallas_call(
        body,
        out_shape=out_shape,
        grid=grid,
        in_specs=[*row_specs, _resident((1, d)), _resident(w_gu.shape, layer), _resident(w_down.shape, layer)],
        out_specs=out_spec,
        scratch_shapes=[pltpu.VMEM((tm, d), BF16), pltpu.VMEM((tm, d), F32)],
        compiler_params=pltpu.CompilerParams(
            dimension_semantics=("parallel",) * len(grid), vmem_limit_bytes=VMEM_LIMIT),
        name="ffn",
    )(*row_operands, gain.reshape(1, d), w_gu, w_down)


def _ffn(h, gain, w_gu, w_down, layer, mixer=None):
    t, d = h.shape
    tm = _row_tile(t)
    rows = pl.BlockSpec((tm, d), lambda i: (i, 0))
    if mixer is None:
        body, operands, specs = _ffn_body, [h], [rows]
    else:
        o, w_o, mixer_layer = mixer
        body, operands = _ffn_mixed_body, [h, o, w_o]
        specs = [rows, pl.BlockSpec((tm, o.shape[1]), lambda i: (i, 0)), _resident(w_o.shape, mixer_layer)]
    return _ffn_call(body, (t // tm,), operands, specs, gain, w_gu, w_down, layer,
                     jax.ShapeDtypeStruct((t, d), F32), rows, tm)


def _ffn_first(x, meta, gain, w_gu, w_down, layer):
    batch, s, d = x.shape
    n_meta = meta.shape[0]
    seq = n_meta + s
    tm = _row_tile(seq)
    nt = seq // tm
    assert n_meta % 8 == 0 and n_meta < tm
    window = pl.BlockSpec((pl.Element(1), pl.Element(tm), pl.Element(d)),
                          lambda b, t: (b, pl.multiple_of(jnp.maximum(t * tm - n_meta, 0), 8), 0))
    return _ffn_call(_ffn_first_body, (batch, nt), [x, meta], [window, _resident(meta.shape)],
                     gain, w_gu, w_down, layer, jax.ShapeDtypeStruct((batch * seq, d), F32),
                     pl.BlockSpec((tm, d), lambda b, t: (b * nt + t, 0)), tm)


def _ffn_last(h, gain, w_gu, w_down, layer, mixer, batch, n_meta):
    t, d = h.shape
    seq = t // batch
    s = seq - n_meta
    tm = _row_tile(s, 512)
    o, w_o, mixer_layer = mixer
    k = o.shape[1]

    def window(width):
        return pl.BlockSpec((pl.Element(1), pl.Element(tm), pl.Element(width)),
                            lambda b, i: (b, pl.multiple_of(n_meta + i * tm, 8), 0))

    return _ffn_call(_ffn_last_body, (batch, s // tm),
                     [h.reshape(batch, seq, d), o.reshape(batch, seq, k), w_o],
                     [window(d), window(k), _resident(w_o.shape, mixer_layer)],
                     gain, w_gu, w_down, layer, jax.ShapeDtypeStruct((batch, s, d), F32),
                     pl.BlockSpec((None, tm, d), lambda b, i: (b, i, 0)), tm)


def _sb_proj_body(h_ref, g_ref, w_ref, hg_ref, seg_ref, o_ref):
    n = o_ref.shape[2]
    xn = _rmsnorm_rows(h_ref[...], g_ref[...]).astype(BF16)
    seg = seg_ref[...]
    width = seg_ref.shape[0]
    for part in range(2):
        y = jnp.dot(xn, w_ref[:, part * n:(part + 1) * n], preferred_element_type=F32)
        for c in range(n // width):
            cols = slice(c * width, (c + 1) * width)
            yc = y[:, cols]
            ms = jnp.dot((yc * yc).astype(BF16), seg, preferred_element_type=F32)
            o_ref[part, :, cols] = (yc * lax.rsqrt(ms + EPS) * hg_ref[part, :, cols]).astype(BF16)
    o_ref[2] = jnp.dot(xn, w_ref[:, 2 * n:], preferred_element_type=F32).astype(BF16)


def _sb_proj(h, gain, w_qkv, layer, head_gains, seg):
    t, d = h.shape
    n = w_qkv.shape[2] // 3
    tm = _row_tile(t)
    return pl.pallas_call(
        _sb_proj_body,
        out_shape=jax.ShapeDtypeStruct((3, t, n), BF16),
        grid=(t // tm,),
        in_specs=[
            pl.BlockSpec((tm, d), lambda i: (i, 0)),
            _resident((1, d)),
            _resident(w_qkv.shape, layer),
            _resident(head_gains.shape),
            _resident(seg.shape),
        ],
        out_specs=pl.BlockSpec((3, tm, n), lambda i: (0, i, 0)),
        compiler_params=pltpu.CompilerParams(
            dimension_semantics=("parallel",), vmem_limit_bytes=VMEM_LIMIT),
        name="sb_proj",
    )(h, gain.reshape(1, d), w_qkv, head_gains, seg)


def _sb_attn_body(q_ref, k_ref, v_ref, cum_ref, o_ref, qp_ref, kp_ref, vp_ref, carry_ref, acc_ref,
                  *, seq):
    rows, tile = SB_ROWS, SB_TILE
    front = tile - rows
    nw = -(-seq // rows)
    lp = nw * rows
    big = jnp.float32(3.0e38)

    qp_ref[0:seq, :] = q_ref[0]
    kp_ref[front:front + seq, :] = k_ref[0]
    vp_ref[front:front + seq, :] = v_ref[0]
    for dst, lo, hi in ((qp_ref, seq, lp), (kp_ref, 0, front), (vp_ref, 0, front),
                        (kp_ref, front + seq, front + lp), (vp_ref, front + seq, front + lp)):
        if hi > lo:
            dst[lo:hi, :] = jnp.zeros((hi - lo, LANES), dst.dtype)

    lane = lax.broadcasted_iota(jnp.int32, (1, LANES), 1)
    head0 = lane < SB_HEAD_DIM
    row = lax.broadcasted_iota(jnp.int32, (2 * rows, LANES), 0)
    col = lax.broadcasted_iota(jnp.int32, (2 * rows, LANES), 1)
    causal = col < jnp.where(row >= rows, row - rows, row) + (LANES - rows)

    def mask_own(x):
        return jnp.concatenate([x[:, :tile - LANES], jnp.where(causal, x[:, tile - LANES:], 0.0)], axis=1)

    def logits(q0, kt, masked):
        q = qp_ref[pl.ds(q0, rows), :]
        zero = jnp.zeros_like(q)
        q2 = jnp.concatenate([jnp.where(head0, q, zero), jnp.where(head0, zero, q)], axis=0)
        z = lax.dot_general(q2, kp_ref[pl.ds(kt, tile), :], _NT, preferred_element_type=F32)
        sp = _softplus(z)
        if masked:
            sp = mask_own(sp)
        return z, sp.astype(BF16), vp_ref[pl.ds(kt, tile), :]

    def suffix_sums(sps):
        cr = jnp.dot(jnp.concatenate(sps, axis=0), cum_ref[...], preferred_element_type=F32)
        return [cr[u * 2 * rows:(u + 1) * 2 * rows] for u in range(len(sps))]

    def weigh(z, later, vblk, state, masked):
        if state is not None:
            carry = jnp.broadcast_to(state[0][:, 0:1], (2 * rows, LANES))
            later = later + jnp.concatenate([carry] * (tile // LANES), axis=1)
        w = jnp.exp(z - later)
        if masked:
            w = mask_own(w)
        acc = jnp.dot(w.astype(BF16), vblk, preferred_element_type=F32)
        if state is not None:
            acc = acc + state[1]
        return later[:, :LANES], acc

    def emit(i, acc):
        n_rows = min(rows, seq - i * rows)
        out = jnp.where(head0, acc[:rows], acc[rows:])
        o_ref[0, i * rows:i * rows + n_rows, :] = out[:n_rows].astype(o_ref.dtype)

    staged = [logits(i * rows, i * rows, True) for i in range(nw)]
    sums = suffix_sums([sp for _, sp, _ in staged])
    low = big
    for i in range(nw):
        z, _, vblk = staged[i]
        carry_ref[i], acc_ref[i] = sums_i, acc = weigh(z, sums[i], vblk, None, True)
        emit(i, acc)
        if i * rows >= tile:
            low = jnp.minimum(low, jnp.min(sums_i[:, 0:1]))
    carry_ref[nw] = jnp.zeros(carry_ref.shape[1:], F32)
    acc_ref[nw] = jnp.zeros(acc_ref.shape[1:], F32)

    def later_step(loop_state):
        d, _ = loop_state
        slots, nexts, staged = [], [], []
        for i in range(nw):
            valid = i * rows >= d * tile
            slots.append(jnp.where(valid, i, nw))
            nexts.append(i * rows >= (d + 1) * tile)
            kt = pl.multiple_of(jnp.maximum(i * rows - d * tile, 0), rows)
            staged.append(logits(i * rows, kt, False))
        sums = suffix_sums([sp for _, sp, _ in staged])
        low = big
        for i in range(nw):
            z, _, vblk = staged[i]
            sums_i, acc = weigh(z, sums[i], vblk, (carry_ref[slots[i]], acc_ref[slots[i]]), False)
            carry_ref[slots[i]] = sums_i
            acc_ref[slots[i]] = acc
            low = jnp.minimum(low, jnp.where(nexts[i], jnp.min(sums_i[:, 0:1]), big))
        return d + 1, low

    steps, _ = lax.while_loop(lambda loop_state: loop_state[1] < SB_SKIP_LOG, later_step,
                              (jnp.int32(1), low))

    @pl.when(steps > 1)
    def _():
        for i in range(nw):
            emit(i, acc_ref[i])


def _sb_attn(qkv, cum, batch, seq):
    n = qkv.shape[2]
    assert seq % 16 == 0 and n % LANES == 0 and SB_TILE % SB_ROWS == 0
    qkv = qkv.reshape(3, batch, seq, n)
    nw = -(-seq // SB_ROWS)
    body = functools.partial(_sb_attn_body, seq=seq)

    def spec(which):
        return pl.BlockSpec((None, 1, seq, LANES), lambda b, p, which=which: (which, b, 0, p))

    q_pad = pltpu.VMEM((nw * SB_ROWS, LANES), BF16)
    kv_pad = pltpu.VMEM((SB_TILE - SB_ROWS + nw * SB_ROWS, LANES), BF16)
    state = pltpu.VMEM((nw + 1, 2 * SB_ROWS, LANES), F32)
    out = pl.pallas_call(
        body,
        out_shape=jax.ShapeDtypeStruct((batch, seq, n), BF16),
        grid=(batch, n // LANES),
        in_specs=[spec(0), spec(1), spec(2),
                  pl.BlockSpec((SB_TILE, SB_TILE), lambda b, p: (0, 0))],
        out_specs=pl.BlockSpec((1, seq, LANES), lambda b, p: (b, 0, p)),
        scratch_shapes=[q_pad, kv_pad, kv_pad, state, state],
        compiler_params=pltpu.CompilerParams(
            dimension_semantics=("parallel", "parallel"), vmem_limit_bytes=VMEM_LIMIT),
        name="sb_attn",
    )(qkv, qkv, qkv, cum)
    return out.reshape(batch * seq, n)


def _sb_constants():
    r = jnp.arange(SEG_WIDTH)[:, None]
    c = jnp.arange(SEG_WIDTH)[None, :]
    seg = jnp.where(r // SB_HEAD_DIM == c // SB_HEAD_DIM, 1.0 / SB_HEAD_DIM, 0.0).astype(BF16)
    cum = (jnp.arange(SB_TILE)[:, None] >= jnp.arange(SB_TILE)[None, :]).astype(BF16)
    return seg, cum


def _sb_mixer(h, gain, w_qkv, g_q, g_k, layer, batch, seq):
    seg, cum = _sb_constants()
    scale = SB_HEAD_DIM ** -0.5
    head_gains = jnp.stack([jnp.tile(g_q, SB_HEADS) * scale, jnp.tile(g_k, SB_HEADS)])[:, None, :]
    qkv = _sb_proj(h, gain, w_qkv, layer, head_gains.astype(F32), seg)
    return _sb_attn(qkv, cum, batch, seq)


def _gla_proj_body(h_ref, g_ref, w_ref, wlow_ref, wup_ref, b_ref, o_ref, la_ref):
    xn = _rmsnorm_rows(h_ref[...], g_ref[...]).astype(BF16)
    g_low = jnp.dot(xn, wlow_ref[...], preferred_element_type=F32).astype(BF16)
    pre = jnp.dot(g_low, wup_ref[...], preferred_element_type=F32) + b_ref[...]
    la_ref[...] = (jnp.minimum(pre, 0.0) - jnp.log(1.0 + jnp.exp(-jnp.abs(pre)))) / GLA_TAU
    o_ref[...] = jnp.dot(xn, w_ref[:, :o_ref.shape[1]], preferred_element_type=F32)


def _gla_proj(h, gain, w_in, w_low, w_up, layer, b_gate):
    t, d = h.shape
    dk = w_up.shape[2]
    n = w_in.shape[2] - GLA_GATE_RANK
    tm = _row_tile(t)
    return pl.pallas_call(
        _gla_proj_body,
        out_shape=(jax.ShapeDtypeStruct((t, n), F32), jax.ShapeDtypeStruct((t, dk), F32)),
        grid=(t // tm,),
        in_specs=[
            pl.BlockSpec((tm, d), lambda i: (i, 0)),
            _resident((1, d)),
            _resident(w_in.shape, layer),
            _resident(w_low.shape, layer),
            _resident(w_up.shape, layer),
            _resident((1, dk)),
        ],
        out_specs=(pl.BlockSpec((tm, n), lambda i: (i, 0)),
                   pl.BlockSpec((tm, dk), lambda i: (i, 0))),
        compiler_params=pltpu.CompilerParams(
            dimension_semantics=("parallel",), vmem_limit_bytes=VMEM_LIMIT),
        name="gla_proj",
    )(h, gain.reshape(1, d), w_in, w_low, w_up, b_gate.reshape(1, dk))


def _gla_core_body(q_ref, k_ref, v_ref, r_ref, la_ref, gout_ref, tril_ref, o_ref,
                   st_ref, oi_ref, kv_ref, qd_ref, dec_ref, *, seq, hk, group):
    c = GLA_CHUNK
    lead = seq % c
    n_full = seq // c
    first = 1 if lead else 0
    scale = hk ** -0.5
    row = lax.broadcasted_iota(jnp.int32, (c, c), 0)
    col = lax.broadcasted_iota(jnp.int32, (c, c), 1)
    causal = col <= row

    def local(chunks):
        decay = []
        for _, _, _, _, a in chunks:
            hi, lo = _split_hi_lo(a)
            decay.append(jnp.dot(tril_ref[...], jnp.concatenate([hi, lo], axis=0),
                                 preferred_element_type=F32))
        atts, kss = [], []
        for (slot, q, k, _, _), b in zip(chunks, decay):
            b_last = b[c - 1:c, :]
            qd = (q * scale * jnp.exp(b)).astype(BF16)
            kd = (k * jnp.exp(-b)).astype(BF16)
            kss.append((k * jnp.exp(b_last - b)).astype(BF16))
            atts.append(lax.dot_general(qd, kd, _NT, preferred_element_type=F32))
            qd_ref[slot] = qd
            dec_ref[slot] = jnp.broadcast_to(jnp.exp(b_last), dec_ref.shape[1:])
        for (slot, _, _, v, _), att, ks in zip(chunks, atts, kss):
            att = jnp.where(causal, att, 0.0).astype(BF16)
            oi_ref[slot] = jnp.dot(att, v.astype(BF16), preferred_element_type=F32)
            kv_ref[slot] = jnp.dot(v.T.astype(BF16), ks, preferred_element_type=F32)

    def carried(slot, r):
        st = st_ref[...]
        o = oi_ref[slot] + lax.dot_general(qd_ref[slot], st.astype(BF16), _NT, preferred_element_type=F32)
        st_ref[...] = st * dec_ref[slot][0:1, :] + kv_ref[slot]
        o = o * lax.rsqrt(jnp.mean(o * o, axis=-1, keepdims=True) + EPS) * gout_ref[...]
        return (o * (r * jax.nn.sigmoid(r))).astype(o_ref.dtype)

    def padded(ref):
        x = ref[0, 0:lead, :]
        return jnp.concatenate([jnp.zeros((c - lead, x.shape[1]), x.dtype), x], axis=0)

    def rows_of(i):
        return pl.ds(pl.multiple_of(lead + i * c, 16), c)

    def chunk_inputs(i):
        rows = rows_of(i)
        return (first + i, q_ref[0, rows, :], k_ref[0, rows, :], v_ref[0, rows, :], la_ref[0, rows, :])

    def local_group(gi, carry):
        local([chunk_inputs(gi * group + u) for u in range(group)])
        return carry

    def carried_step(i, carry):
        rows = rows_of(i)
        o_ref[0, rows, :] = carried(first + i, r_ref[0, rows, :])
        return carry

    assert n_full % group == 0
    n_groups = n_full // group
    head = [(0, padded(q_ref), padded(k_ref), padded(v_ref), padded(la_ref))] if lead else []
    local(head + [chunk_inputs(u) for u in range(group)])
    lax.fori_loop(1, n_groups, local_group, 0)

    st_ref[...] = jnp.zeros_like(st_ref)
    if lead:
        o_ref[0, 0:lead, :] = carried(0, padded(r_ref))[c - lead:, :]
    for u in range(group):
        carried_step(u, 0)
    lax.fori_loop(group, n_full, carried_step, 0, unroll=group)


def _gla_core(proj, la, g_out, batch, seq, dk, dv, *, group=16):
    hk, hv = dk // GLA_HEADS, dv // GLA_HEADS
    assert hk % LANES == 0 and hv % LANES == 0 and seq % 16 == 0 and (seq % GLA_CHUNK) % 16 == 0
    proj = proj.reshape(batch, seq, proj.shape[1])
    la = la.reshape(batch, seq, dk)
    tril = (jnp.arange(GLA_CHUNK)[:, None] >= jnp.arange(2 * GLA_CHUNK)[None, :] % GLA_CHUNK).astype(BF16)
    slots = -(-seq // GLA_CHUNK)
    group = group if (seq // GLA_CHUNK) % group == 0 else 1
    body = functools.partial(_gla_core_body, seq=seq, hk=hk, group=group)
    nk, nv = dk // hk, dv // hv
    out = pl.pallas_call(
        body,
        out_shape=jax.ShapeDtypeStruct((batch, seq, dv), BF16),
        grid=(batch, GLA_HEADS),
        in_specs=[
            pl.BlockSpec((1, seq, hk), lambda b, h: (b, 0, h)),
            pl.BlockSpec((1, seq, hk), lambda b, h: (b, 0, nk + h)),
            pl.BlockSpec((1, seq, hv), lambda b, h: (b, 0, (2 * dk) // hv + h)),
            pl.BlockSpec((1, seq, hv), lambda b, h: (b, 0, (2 * dk) // hv + nv + h)),
            pl.BlockSpec((1, seq, hk), lambda b, h: (b, 0, h)),
            pl.BlockSpec((1, hv), lambda b, h: (0, h)),
            pl.BlockSpec((GLA_CHUNK, 2 * GLA_CHUNK), lambda b, h: (0, 0)),
        ],
        out_specs=pl.BlockSpec((1, seq, hv), lambda b, h: (b, 0, h)),
        scratch_shapes=[
            pltpu.VMEM((hv, hk), F32),
            pltpu.VMEM((slots, GLA_CHUNK, hv), F32),
            pltpu.VMEM((slots, hv, hk), F32),
            pltpu.VMEM((slots, GLA_CHUNK, hk), BF16),
            pltpu.VMEM((slots, 8, hk), F32),
        ],
        compiler_params=pltpu.CompilerParams(
            dimension_semantics=("parallel", "parallel"), vmem_limit_bytes=VMEM_LIMIT),
        name="gla_core",
    )(proj, proj, proj, proj, la, g_out.reshape(1, dv), tril)
    return out.reshape(batch * seq, dv)


def _gla_mixer(h, gain, w_in, w_low, w_up, b_gate, g_out, layer, batch, seq):
    dk = w_up.shape[2]
    dv = g_out.shape[0]
    proj, la = _gla_proj(h, gain, w_in, w_low, w_up, layer, b_gate)
    return _gla_core(proj, la, g_out, batch, seq, dk, dv)


def kernel(x, meta, ffn_a_norm, ffn_a_w_gu, ffn_a_w_down, mix_norm, sb_w_qkv, sb_q_norm, sb_k_norm, sb_w_o, gla_w_in, gla_w_gate_up, gla_b_gate, gla_out_norm, gla_w_o, ffn_b_norm, ffn_b_w_gu, ffn_b_w_down):
    batch, s, d = x.shape
    depth = ffn_a_norm.shape[0]
    n_meta = meta.shape[0]
    seq = n_meta + s
    bf = lambda w: w.astype(BF16)
    a_gu, a_down, b_gu, b_down = bf(ffn_a_w_gu), bf(ffn_a_w_down), bf(ffn_b_w_gu), bf(ffn_b_w_down)
    w_qkv, w_sbo, w_in, w_glao = bf(sb_w_qkv), bf(sb_w_o), bf(gla_w_in), bf(gla_w_o)
    low_pad = LANES - GLA_GATE_RANK
    w_low = jnp.pad(w_in[:, :, w_in.shape[2] - GLA_GATE_RANK:], ((0, 0), (0, 0), (0, low_pad)))
    w_up = jnp.pad(bf(gla_w_gate_up), ((0, 0), (0, low_pad), (0, 0)))
    h = None
    for i in range(depth):
        if i == 0:
            h = _ffn_first(x, meta.astype(x.dtype), ffn_a_norm[i], a_gu, a_down, i)
        else:
            h = _ffn(h, ffn_a_norm[i], a_gu, a_down, i)
        j = i // 2
        if i % 2 == 0:
            o = _sb_mixer(h, mix_norm[i], w_qkv, sb_q_norm[j], sb_k_norm[j], j, batch, seq)
            w_o = w_sbo
        else:
            o = _gla_mixer(h, mix_norm[i], w_in, w_low, w_up, gla_b_gate[j], gla_out_norm[j], j, batch, seq)
            w_o = w_glao
        if i < depth - 1:
            h = _ffn(h, ffn_b_norm[i], b_gu, b_down, i, mixer=(o, w_o, j))
        else:
            h = _ffn_last(h, ffn_b_norm[i], b_gu, b_down, i, (o, w_o, j), batch, n_meta)
    return h
```

```python
import functools

import jax
import jax.numpy as jnp
from jax import lax
from jax.experimental import pallas as pl
from jax.experimental.pallas import tpu as pltpu

F32 = jnp.float32
BF16 = jnp.bfloat16

N_META = 16
RES_HALF = 0.5
EPS = 1e-6
SB_HEADS = 16
SB_HEAD_DIM = 64
GLA_HEADS = 4
GLA_GATE_RANK = 16
GLA_TAU = 16.0
GLA_CHUNK = 64

LANES = 128
SB_ROWS = 64
SB_TILE = 256
SEG_WIDTH = 256
SB_SKIP_LOG = 104.0
VMEM_LIMIT = 56 * 1024 * 1024

_NT = (((1,), (1,)), ((), ()))


def _row_tile(t, limit=1024):
    best = None
    for cand in range(16, min(t, limit) + 1, 16):
        if t % cand == 0:
            best = cand
    assert best is not None, t
    return best


def _split_hi_lo(a):
    hi = a.astype(BF16)
    lo = (a - hi.astype(F32)).astype(BF16)
    return hi, lo


def _rmsnorm_rows(x, g):
    ms = jnp.mean(x * x, axis=-1, keepdims=True)
    return x * lax.rsqrt(ms + EPS) * g


def _softplus(z):
    return jnp.maximum(z, 0.0) + jnp.log(1.0 + jnp.exp(-jnp.abs(z)))


FFN_TILE = 256


def _resident(shape, layer=None):
    if layer is None:
        return pl.BlockSpec(shape, lambda *_: (0,) * len(shape), pipeline_mode=pl.Buffered(1))
    return pl.BlockSpec((None, *shape[1:]), lambda *_: (layer,) + (0,) * (len(shape) - 1),
                        pipeline_mode=pl.Buffered(1))


def _ffn_rows(base_ref, g_ref, wgu_ref, wd_ref, o_ref, xn_ref, acc_ref):
    dff = wd_ref.shape[0]
    tf = FFN_TILE
    nf = dff // tf
    xn_ref[...] = _rmsnorm_rows(base_ref[...], g_ref[...]).astype(BF16)
    for j in range(nf):
        xn = xn_ref[...]
        g = jnp.dot(xn, wgu_ref[:, j * tf:(j + 1) * tf], preferred_element_type=F32)
        u = jnp.dot(xn, wgu_ref[:, dff + j * tf:dff + (j + 1) * tf], preferred_element_type=F32)
        a = (g * jax.nn.sigmoid(g) * u).astype(BF16)
        part = jnp.dot(a, wd_ref[j * tf:(j + 1) * tf, :], preferred_element_type=F32)
        if j == 0:
            acc_ref[...] = part
        elif j < nf - 1:
            acc_ref[...] += part
        else:
            o_ref[...] = base_ref[...] + RES_HALF * (acc_ref[...] + part)


def _ffn_body(h_ref, *refs):
    _ffn_rows(h_ref, *refs)


def _ffn_mixed_body(h_ref, mo_ref, wo_ref, g_ref, wgu_ref, wd_ref, o_ref, xn_ref, acc_ref):
    o_ref[...] = h_ref[...] + jnp.dot(mo_ref[...], wo_ref[...].astype(BF16), preferred_element_type=F32)
    _ffn_rows(o_ref, g_ref, wgu_ref, wd_ref, o_ref, xn_ref, acc_ref)


def _ffn_last_body(h_ref, mo_ref, wo_ref, g_ref, wgu_ref, wd_ref, o_ref, xn_ref, acc_ref):
    o_ref[...] = h_ref[0] + jnp.dot(mo_ref[0], wo_ref[...].astype(BF16), preferred_element_type=F32)
    _ffn_rows(o_ref, g_ref, wgu_ref, wd_ref, o_ref, xn_ref, acc_ref)


def _ffn_first_body(x_ref, meta_ref, g_ref, wgu_ref, wd_ref, o_ref, xn_ref, acc_ref):
    n_meta = meta_ref.shape[0]
    xt = x_ref[0]
    lead = jnp.concatenate([meta_ref[...], xt[:xt.shape[0] - n_meta]], axis=0)
    o_ref[...] = jnp.where(pl.program_id(1) == 0, lead, xt)
    _ffn_rows(o_ref, g_ref, wgu_ref, wd_ref, o_ref, xn_ref, acc_ref)


def _ffn_call(body, grid, row_operands, row_specs, gain, w_gu, w_down, layer, out_shape, out_spec, tm):
    d = gain.shape[0]
    assert w_down.shape[1] % FFN_TILE == 0
    return pl.pallas_call(
        body,
        out_shape=out_shape,
        grid=grid,
        in_specs=[*row_specs, _resident((1, d)), _resident(w_gu.shape, layer), _resident(w_down.shape, layer)],
        out_specs=out_spec,
        scratch_shapes=[pltpu.VMEM((tm, d), BF16), pltpu.VMEM((tm, d), F32)],
        compiler_params=pltpu.CompilerParams(
            dimension_semantics=("parallel",) * len(grid), vmem_limit_bytes=VMEM_LIMIT),
        name="ffn",
    )(*row_operands, gain.reshape(1, d), w_gu, w_down)


def _ffn(h, gain, w_gu, w_down, layer, mixer=None):
    t, d = h.shape
    tm = _row_tile(t)
    rows = pl.BlockSpec((tm, d), lambda i: (i, 0))
    if mixer is None:
        body, operands, specs = _ffn_body, [h], [rows]
    else:
        o, w_o, mixer_layer = mixer
        body, operands = _ffn_mixed_body, [h, o, w_o]
        specs = [rows, pl.BlockSpec((tm, o.shape[1]), lambda i: (i, 0)), _resident(w_o.shape, mixer_layer)]
    return _ffn_call(body, (t // tm,), operands, specs, gain, w_gu, w_down, layer,
                     jax.ShapeDtypeStruct((t, d), F32), rows, tm)


def _ffn_first(x, meta, gain, w_gu, w_down, layer):
    batch, s, d = x.shape
    n_meta = meta.shape[0]
    seq = n_meta + s
    tm = _row_tile(seq)
    nt = seq // tm
    assert n_meta % 8 == 0 and n_meta < tm
    window = pl.BlockSpec((pl.Element(1), pl.Element(tm), pl.Element(d)),
                          lambda b, t: (b, pl.multiple_of(jnp.maximum(t * tm - n_meta, 0), 8), 0))
    return _ffn_call(_ffn_first_body, (batch, nt), [x, meta], [window, _resident(meta.shape)],
                     gain, w_gu, w_down, layer, jax.ShapeDtypeStruct((batch * seq, d), F32),
                     pl.BlockSpec((tm, d), lambda b, t: (b * nt + t, 0)), tm)


def _ffn_last(h, gain, w_gu, w_down, layer, mixer, batch, n_meta):
    t, d = h.shape
    seq = t // batch
    s = seq - n_meta
    tm = _row_tile(s, 512)
    o, w_o, mixer_layer = mixer
    k = o.shape[1]

    def window(width):
        return pl.BlockSpec((pl.Element(1), pl.Element(tm), pl.Element(width)),
                            lambda b, i: (b, pl.multiple_of(n_meta + i * tm, 8), 0))

    return _ffn_call(_ffn_last_body, (batch, s // tm),
                     [h.reshape(batch, seq, d), o.reshape(batch, seq, k), w_o],
                     [window(d), window(k), _resident(w_o.shape, mixer_layer)],
                     gain, w_gu, w_down, layer, jax.ShapeDtypeStruct((batch, s, d), F32),
                     pl.BlockSpec((None, tm, d), lambda b, i: (b, i, 0)), tm)


def _sb_proj_body(h_ref, g_ref, w_ref, hg_ref, seg_ref, o_ref):
    n = o_ref.shape[2]
    xn = _rmsnorm_rows(h_ref[...], g_ref[...]).astype(BF16)
    seg = seg_ref[...]
    width = seg_ref.shape[0]
    for part in range(2):
        y = jnp.dot(xn, w_ref[:, part * n:(part + 1) * n].astype(BF16), preferred_element_type=F32)
        for c in range(n // width):
            cols = slice(c * width, (c + 1) * width)
            yc = y[:, cols]
            ms = jnp.dot((yc * yc).astype(BF16), seg, preferred_element_type=F32)
            o_ref[part, :, cols] = (yc * lax.rsqrt(ms + EPS) * hg_ref[part, :, cols]).astype(BF16)
    o_ref[2] = jnp.dot(xn, w_ref[:, 2 * n:].astype(BF16), preferred_element_type=F32).astype(BF16)


def _sb_proj(h, gain, w_qkv, layer, head_gains, seg):
    t, d = h.shape
    n = w_qkv.shape[2] // 3
    tm = _row_tile(t)
    return pl.pallas_call(
        _sb_proj_body,
        out_shape=jax.ShapeDtypeStruct((3, t, n), BF16),
        grid=(t // tm,),
        in_specs=[
            pl.BlockSpec((tm, d), lambda i: (i, 0)),
            _resident((1, d)),
            _resident(w_qkv.shape, layer),
            _resident(head_gains.shape),
            _resident(seg.shape),
        ],
        out_specs=pl.BlockSpec((3, tm, n), lambda i: (0, i, 0)),
        compiler_params=pltpu.CompilerParams(
            dimension_semantics=("parallel",), vmem_limit_bytes=VMEM_LIMIT),
        name="sb_proj",
    )(h, gain.reshape(1, d), w_qkv, head_gains, seg)


def _sb_attn_body(q_ref, k_ref, v_ref, cum_ref, o_ref, qp_ref, kp_ref, vp_ref, carry_ref, acc_ref,
                  *, seq):
    rows, tile = SB_ROWS, SB_TILE
    front = tile - rows
    nw = -(-seq // rows)
    lp = nw * rows
    big = jnp.float32(3.0e38)

    qp_ref[0:seq, :] = q_ref[0]
    kp_ref[front:front + seq, :] = k_ref[0]
    vp_ref[front:front + seq, :] = v_ref[0]
    for dst, lo, hi in ((qp_ref, seq, lp), (kp_ref, 0, front), (vp_ref, 0, front),
                        (kp_ref, front + seq, front + lp), (vp_ref, front + seq, front + lp)):
        if hi > lo:
            dst[lo:hi, :] = jnp.zeros((hi - lo, LANES), dst.dtype)

    lane = lax.broadcasted_iota(jnp.int32, (1, LANES), 1)
    head0 = lane < SB_HEAD_DIM
    row = lax.broadcasted_iota(jnp.int32, (2 * rows, LANES), 0)
    col = lax.broadcasted_iota(jnp.int32, (2 * rows, LANES), 1)
    causal = col < jnp.where(row >= rows, row - rows, row) + (LANES - rows)

    def mask_own(x):
        return jnp.concatenate([x[:, :tile - LANES], jnp.where(causal, x[:, tile - LANES:], 0.0)], axis=1)

    def logits(q0, kt, masked):
        q = qp_ref[pl.ds(q0, rows), :]
        zero = jnp.zeros_like(q)
        q2 = jnp.concatenate([jnp.where(head0, q, zero), jnp.where(head0, zero, q)], axis=0)
        z = lax.dot_general(q2, kp_ref[pl.ds(kt, tile), :], _NT, preferred_element_type=F32)
        sp = _softplus(z)
        if masked:
            sp = mask_own(sp)
        return z, sp.astype(BF16), vp_ref[pl.ds(kt, tile), :]

    def suffix_sums(sps):
        cr = jnp.dot(jnp.concatenate(sps, axis=0), cum_ref[...], preferred_element_type=F32)
        return [cr[u * 2 * rows:(u + 1) * 2 * rows] for u in range(len(sps))]

    def weigh(z, later, vblk, state, masked):
        if state is not None:
            carry = jnp.broadcast_to(state[0][:, 0:1], (2 * rows, LANES))
            later = later + jnp.concatenate([carry] * (tile // LANES), axis=1)
        w = jnp.exp(z - later)
        if masked:
            w = mask_own(w)
        acc = jnp.dot(w.astype(BF16), vblk, preferred_element_type=F32)
        if state is not None:
            acc = acc + state[1]
        return later[:, :LANES], acc

    def emit(i, acc):
        n_rows = min(rows, seq - i * rows)
        out = jnp.where(head0, acc[:rows], acc[rows:])
        o_ref[0, i * rows:i * rows + n_rows, :] = out[:n_rows].astype(o_ref.dtype)

    staged = [logits(i * rows, i * rows, True) for i in range(nw)]
    sums = suffix_sums([sp for _, sp, _ in staged])
    low = big
    for i in range(nw):
        z, _, vblk = staged[i]
        carry_ref[i], acc_ref[i] = sums_i, acc = weigh(z, sums[i], vblk, None, True)
        emit(i, acc)
        if i * rows >= tile:
            low = jnp.minimum(low, jnp.min(sums_i[:, 0:1]))
    carry_ref[nw] = jnp.zeros(carry_ref.shape[1:], F32)
    acc_ref[nw] = jnp.zeros(acc_ref.shape[1:], F32)

    def later_step(loop_state):
        d, _ = loop_state
        slots, nexts, staged = [], [], []
        for i in range(nw):
            valid = i * rows >= d * tile
            slots.append(jnp.where(valid, i, nw))
            nexts.append(i * rows >= (d + 1) * tile)
            kt = pl.multiple_of(jnp.maximum(i * rows - d * tile, 0), rows)
            staged.append(logits(i * rows, kt, False))
        sums = suffix_sums([sp for _, sp, _ in staged])
        low = big
        for i in range(nw):
            z, _, vblk = staged[i]
            sums_i, acc = weigh(z, sums[i], vblk, (carry_ref[slots[i]], acc_ref[slots[i]]), False)
            carry_ref[slots[i]] = sums_i
            acc_ref[slots[i]] = acc
            low = jnp.minimum(low, jnp.where(nexts[i], jnp.min(sums_i[:, 0:1]), big))
        return d + 1, low

    steps, _ = lax.while_loop(lambda loop_state: loop_state[1] < SB_SKIP_LOG, later_step,
                              (jnp.int32(1), low))

    @pl.when(steps > 1)
    def _():
        for i in range(nw):
            emit(i, acc_ref[i])


def _sb_attn(qkv, cum, batch, seq):
    n = qkv.shape[2]
    assert seq % 16 == 0 and n % LANES == 0 and SB_TILE % SB_ROWS == 0
    qkv = qkv.reshape(3, batch, seq, n)
    nw = -(-seq // SB_ROWS)
    body = functools.partial(_sb_attn_body, seq=seq)

    def spec(which):
        return pl.BlockSpec((None, 1, seq, LANES), lambda b, p, which=which: (which, b, 0, p))

    q_pad = pltpu.VMEM((nw * SB_ROWS, LANES), BF16)
    kv_pad = pltpu.VMEM((SB_TILE - SB_ROWS + nw * SB_ROWS, LANES), BF16)
    state = pltpu.VMEM((nw + 1, 2 * SB_ROWS, LANES), F32)
    out = pl.pallas_call(
        body,
        out_shape=jax.ShapeDtypeStruct((batch, seq, n), BF16),
        grid=(batch, n // LANES),
        in_specs=[spec(0), spec(1), spec(2),
                  pl.BlockSpec((SB_TILE, SB_TILE), lambda b, p: (0, 0))],
        out_specs=pl.BlockSpec((1, seq, LANES), lambda b, p: (b, 0, p)),
        scratch_shapes=[q_pad, kv_pad, kv_pad, state, state],
        compiler_params=pltpu.CompilerParams(
            dimension_semantics=("parallel", "parallel"), vmem_limit_bytes=VMEM_LIMIT),
        name="sb_attn",
    )(qkv, qkv, qkv, cum)
    return out.reshape(batch * seq, n)


def _sb_constants():
    r = jnp.arange(SEG_WIDTH)[:, None]
    c = jnp.arange(SEG_WIDTH)[None, :]
    seg = jnp.where(r // SB_HEAD_DIM == c // SB_HEAD_DIM, 1.0 / SB_HEAD_DIM, 0.0).astype(BF16)
    cum = (jnp.arange(SB_TILE)[:, None] >= jnp.arange(SB_TILE)[None, :]).astype(BF16)
    return seg, cum


def _sb_mixer(h, gain, w_qkv, g_q, g_k, layer, batch, seq):
    seg, cum = _sb_constants()
    scale = SB_HEAD_DIM ** -0.5
    head_gains = jnp.stack([jnp.tile(g_q, SB_HEADS) * scale, jnp.tile(g_k, SB_HEADS)])[:, None, :]
    qkv = _sb_proj(h, gain, w_qkv, layer, head_gains.astype(F32), seg)
    return _sb_attn(qkv, cum, batch, seq)


def _gla_proj_body(h_ref, g_ref, w_ref, wlow_ref, wup_ref, b_ref, o_ref, la_ref):
    xn = _rmsnorm_rows(h_ref[...], g_ref[...]).astype(BF16)
    g_low = jnp.dot(xn, wlow_ref[...].astype(BF16), preferred_element_type=F32).astype(BF16)
    pre = jnp.dot(g_low, wup_ref[...].astype(BF16), preferred_element_type=F32) + b_ref[...]
    la_ref[...] = (jnp.minimum(pre, 0.0) - jnp.log(1.0 + jnp.exp(-jnp.abs(pre)))) / GLA_TAU
    o_ref[...] = jnp.dot(xn, w_ref[:, :o_ref.shape[1]].astype(BF16), preferred_element_type=F32)


def _gla_proj(h, gain, w_in, w_low, w_up, layer, b_gate):
    t, d = h.shape
    dk = w_up.shape[2]
    n = w_in.shape[2] - GLA_GATE_RANK
    tm = _row_tile(t)
    return pl.pallas_call(
        _gla_proj_body,
        out_shape=(jax.ShapeDtypeStruct((t, n), F32), jax.ShapeDtypeStruct((t, dk), F32)),
        grid=(t // tm,),
        in_specs=[
            pl.BlockSpec((tm, d), lambda i: (i, 0)),
            _resident((1, d)),
            _resident(w_in.shape, layer),
            _resident(w_low.shape, layer),
            _resident(w_up.shape, layer),
            _resident((1, dk)),
        ],
        out_specs=(pl.BlockSpec((tm, n), lambda i: (i, 0)),
                   pl.BlockSpec((tm, dk), lambda i: (i, 0))),
        compiler_params=pltpu.CompilerParams(
            dimension_semantics=("parallel",), vmem_limit_bytes=VMEM_LIMIT),
        name="gla_proj",
    )(h, gain.reshape(1, d), w_in, w_low, w_up, b_gate.reshape(1, dk))


def _gla_core_body(q_ref, k_ref, v_ref, r_ref, la_ref, gout_ref, tril_ref, o_ref,
                   st_ref, oi_ref, kv_ref, qd_ref, dec_ref, *, seq, hk, group):
    c = GLA_CHUNK
    lead = seq % c
    n_full = seq // c
    first = 1 if lead else 0
    scale = hk ** -0.5
    row = lax.broadcasted_iota(jnp.int32, (c, c), 0)
    col = lax.broadcasted_iota(jnp.int32, (c, c), 1)
    causal = col <= row

    def local(chunks):
        decay = []
        for _, _, _, _, a in chunks:
            hi, lo = _split_hi_lo(a)
            decay.append(jnp.dot(tril_ref[...], jnp.concatenate([hi, lo], axis=0),
                                 preferred_element_type=F32))
        atts, kss = [], []
        for (slot, q, k, _, _), b in zip(chunks, decay):
            b_last = b[c - 1:c, :]
            qd = (q * scale * jnp.exp(b)).astype(BF16)
            kd = (k * jnp.exp(-b)).astype(BF16)
            kss.append((k * jnp.exp(b_last - b)).astype(BF16))
            atts.append(lax.dot_general(qd, kd, _NT, preferred_element_type=F32))
            qd_ref[slot] = qd
            dec_ref[slot] = jnp.broadcast_to(jnp.exp(b_last), dec_ref.shape[1:])
        for (slot, _, _, v, _), att, ks in zip(chunks, atts, kss):
            att = jnp.where(causal, att, 0.0).astype(BF16)
            oi_ref[slot] = jnp.dot(att, v.astype(BF16), preferred_element_type=F32)
            kv_ref[slot] = jnp.dot(v.T.astype(BF16), ks, preferred_element_type=F32)

    def carried(slot, r):
        st = st_ref[...]
        o = oi_ref[slot] + lax.dot_general(qd_ref[slot], st.astype(BF16), _NT, preferred_element_type=F32)
        st_ref[...] = st * dec_ref[slot][0:1, :] + kv_ref[slot]
        o = o * lax.rsqrt(jnp.mean(o * o, axis=-1, keepdims=True) + EPS) * gout_ref[...]
        return (o * (r * jax.nn.sigmoid(r))).astype(o_ref.dtype)

    def padded(ref):
        x = ref[0, 0:lead, :]
        return jnp.concatenate([jnp.zeros((c - lead, x.shape[1]), x.dtype), x], axis=0)

    def rows_of(i):
        return pl.ds(pl.multiple_of(lead + i * c, 16), c)

    def chunk_inputs(i):
        rows = rows_of(i)
        return (first + i, q_ref[0, rows, :], k_ref[0, rows, :], v_ref[0, rows, :], la_ref[0, rows, :])

    def local_group(gi, carry):
        local([chunk_inputs(gi * group + u) for u in range(group)])
        return carry

    def carried_step(i, carry):
        rows = rows_of(i)
        o_ref[0, rows, :] = carried(first + i, r_ref[0, rows, :])
        return carry

    assert n_full % group == 0
    n_groups = n_full // group
    head = [(0, padded(q_ref), padded(k_ref), padded(v_ref), padded(la_ref))] if lead else []
    local(head + [chunk_inputs(u) for u in range(group)])
    lax.fori_loop(1, n_groups, local_group, 0)

    st_ref[...] = jnp.zeros_like(st_ref)
    if lead:
        o_ref[0, 0:lead, :] = carried(0, padded(r_ref))[c - lead:, :]
    for u in range(group):
        carried_step(u, 0)
    lax.fori_loop(group, n_full, carried_step, 0, unroll=group)


def _gla_core(proj, la, g_out, batch, seq, dk, dv, *, group=16):
    hk, hv = dk // GLA_HEADS, dv // GLA_HEADS
    assert hk % LANES == 0 and hv % LANES == 0 and seq % 16 == 0 and (seq % GLA_CHUNK) % 16 == 0
    proj = proj.reshape(batch, seq, proj.shape[1])
    la = la.reshape(batch, seq, dk)
    tril = (jnp.arange(GLA_CHUNK)[:, None] >= jnp.arange(2 * GLA_CHUNK)[None, :] % GLA_CHUNK).astype(BF16)
    slots = -(-seq // GLA_CHUNK)
    group = group if (seq // GLA_CHUNK) % group == 0 else 1
    body = functools.partial(_gla_core_body, seq=seq, hk=hk, group=group)
    nk, nv = dk // hk, dv // hv
    out = pl.pallas_call(
        body,
        out_shape=jax.ShapeDtypeStruct((batch, seq, dv), BF16),
        grid=(batch, GLA_HEADS),
        in_specs=[
            pl.BlockSpec((1, seq, hk), lambda b, h: (b, 0, h)),
            pl.BlockSpec((1, seq, hk), lambda b, h: (b, 0, nk + h)),
            pl.BlockSpec((1, seq, hv), lambda b, h: (b, 0, (2 * dk) // hv + h)),
            pl.BlockSpec((1, seq, hv), lambda b, h: (b, 0, (2 * dk) // hv + nv + h)),
            pl.BlockSpec((1, seq, hk), lambda b, h: (b, 0, h)),
            pl.BlockSpec((1, hv), lambda b, h: (0, h)),
            pl.BlockSpec((GLA_CHUNK, 2 * GLA_CHUNK), lambda b, h: (0, 0)),
        ],
        out_specs=pl.BlockSpec((1, seq, hv), lambda b, h: (b, 0, h)),
        scratch_shapes=[
            pltpu.VMEM((hv, hk), F32),
            pltpu.VMEM((slots, GLA_CHUNK, hv), F32),
            pltpu.VMEM((slots, hv, hk), F32),
            pltpu.VMEM((slots, GLA_CHUNK, hk), BF16),
            pltpu.VMEM((slots, 8, hk), F32),
        ],
        compiler_params=pltpu.CompilerParams(
            dimension_semantics=("parallel", "parallel"), vmem_limit_bytes=VMEM_LIMIT),
        name="gla_core",
    )(proj, proj, proj, proj, la, g_out.reshape(1, dv), tril)
    return out.reshape(batch * seq, dv)


def _gla_mixer(h, gain, w_in, w_low, w_up, b_gate, g_out, layer, batch, seq):
    dk = w_up.shape[2]
    dv = g_out.shape[0]
    proj, la = _gla_proj(h, gain, w_in, w_low, w_up, layer, b_gate)
    return _gla_core(proj, la, g_out, batch, seq, dk, dv)


def kernel(x, meta, ffn_a_norm, ffn_a_w_gu, ffn_a_w_down, mix_norm, sb_w_qkv, sb_q_norm, sb_k_norm, sb_w_o, gla_w_in, gla_w_gate_up, gla_b_gate, gla_out_norm, gla_w_o, ffn_b_norm, ffn_b_w_gu, ffn_b_w_down):
    batch, s, d = x.shape
    depth = ffn_a_norm.shape[0]
    n_meta = meta.shape[0]
    seq = n_meta + s
    bf = lambda w: w.astype(BF16)
    a_gu, a_down, b_gu, b_down = bf(ffn_a_w_gu), bf(ffn_a_w_down), bf(ffn_b_w_gu), bf(ffn_b_w_down)
    low_pad = LANES - GLA_GATE_RANK
    w_low = jnp.pad(gla_w_in[:, :, gla_w_in.shape[2] - GLA_GATE_RANK:], ((0, 0), (0, 0), (0, low_pad)))
    w_up = jnp.pad(gla_w_gate_up, ((0, 0), (0, low_pad), (0, 0)))
    h = None
    for i in range(depth):
        if i == 0:
            h = _ffn_first(x, meta.astype(x.dtype), ffn_a_norm[i], a_gu, a_down, i)
        else:
            h = _ffn(h, ffn_a_norm[i], a_gu, a_down, i)
        j = i // 2
        if i % 2 == 0:
            o = _sb_mixer(h, mix_norm[i], sb_w_qkv, sb_q_norm[j], sb_k_norm[j], j, batch, seq)
            w_o = sb_w_o
        else:
            o = _gla_mixer(h, mix_norm[i], gla_w_in, w_low, w_up, gla_b_gate[j], gla_out_norm[j], j, batch, seq)
            w_o = gla_w_o
        if i < depth - 1:
            h = _ffn(h, ffn_b_norm[i], b_gu, b_down, i, mixer=(o, w_o, j))
        else:
            h = _ffn_last(h, ffn_b_norm[i], b_gu, b_down, i, (o, w_o, j), batch, n_meta)
    return h
```

```python
import functools

import jax
import jax.numpy as jnp
from jax import lax
from jax.experimental import pallas as pl
from jax.experimental.pallas import tpu as pltpu

F32 = jnp.float32
BF16 = jnp.bfloat16

N_META = 16
RES_HALF = 0.5
EPS = 1e-6
SB_HEADS = 16
SB_HEAD_DIM = 64
GLA_HEADS = 4
GLA_GATE_RANK = 16
GLA_TAU = 16.0
GLA_CHUNK = 64

LANES = 128
SB_ROWS = 64
SB_TILE = 256
SEG_WIDTH = 256
SB_SKIP_LOG = 104.0
VMEM_LIMIT = 56 * 1024 * 1024

_NT = (((1,), (1,)), ((), ()))


def _row_tile(t, limit=1024):
    best = None
    for cand in range(16, min(t, limit) + 1, 16):
        if t % cand == 0:
            best = cand
    assert best is not None, t
    return best


def _split_hi_lo(a):
    hi = a.astype(BF16)
    lo = (a - hi.astype(F32)).astype(BF16)
    return hi, lo


def _rmsnorm_rows(x, g):
    ms = jnp.mean(x * x, axis=-1, keepdims=True)
    return x * lax.rsqrt(ms + EPS) * g


def _softplus(z):
    return jnp.maximum(z, 0.0) + jnp.log(1.0 + jnp.exp(-jnp.abs(z)))


FFN_TILE = 256


def _resident(shape, layer=None):
    if layer is None:
        return pl.BlockSpec(shape, lambda *_: (0,) * len(shape), pipeline_mode=pl.Buffered(1))
    return pl.BlockSpec((None, *shape[1:]), lambda *_: (layer,) + (0,) * (len(shape) - 1),
                        pipeline_mode=pl.Buffered(1))


def _ffn_rows(base_ref, g_ref, wgu_ref, wd_ref, o_ref, xn_ref, acc_ref):
    dff = wd_ref.shape[0]
    tf = FFN_TILE
    nf = dff // tf
    xn_ref[...] = _rmsnorm_rows(base_ref[...], g_ref[...]).astype(BF16)
    for j in range(nf):
        xn = xn_ref[...]
        g = jnp.dot(xn, wgu_ref[:, j * tf:(j + 1) * tf].astype(BF16), preferred_element_type=F32)
        u = jnp.dot(xn, wgu_ref[:, dff + j * tf:dff + (j + 1) * tf].astype(BF16), preferred_element_type=F32)
        a = (g * jax.nn.sigmoid(g) * u).astype(BF16)
        part = jnp.dot(a, wd_ref[j * tf:(j + 1) * tf, :].astype(BF16), preferred_element_type=F32)
        if j == 0:
            acc_ref[...] = part
        elif j < nf - 1:
            acc_ref[...] += part
        else:
            o_ref[...] = base_ref[...] + RES_HALF * (acc_ref[...] + part)


def _ffn_body(h_ref, *refs):
    _ffn_rows(h_ref, *refs)


def _ffn_mixed_body(h_ref, mo_ref, wo_ref, g_ref, wgu_ref, wd_ref, o_ref, xn_ref, acc_ref):
    o_ref[...] = h_ref[...] + jnp.dot(mo_ref[...], wo_ref[...].astype(BF16), preferred_element_type=F32)
    _ffn_rows(o_ref, g_ref, wgu_ref, wd_ref, o_ref, xn_ref, acc_ref)


def _ffn_last_body(h_ref, mo_ref, wo_ref, g_ref, wgu_ref, wd_ref, o_ref, xn_ref, acc_ref):
    o_ref[...] = h_ref[0] + jnp.dot(mo_ref[0], wo_ref[...].astype(BF16), preferred_element_type=F32)
    _ffn_rows(o_ref, g_ref, wgu_ref, wd_ref, o_ref, xn_ref, acc_ref)


def _ffn_first_body(x_ref, meta_ref, g_ref, wgu_ref, wd_ref, o_ref, xn_ref, acc_ref):
    n_meta = meta_ref.shape[0]
    xt = x_ref[0]
    lead = jnp.concatenate([meta_ref[...], xt[:xt.shape[0] - n_meta]], axis=0)
    o_ref[...] = jnp.where(pl.program_id(1) == 0, lead, xt)
    _ffn_rows(o_ref, g_ref, wgu_ref, wd_ref, o_ref, xn_ref, acc_ref)


def _ffn_call(body, grid, row_operands, row_specs, gain, w_gu, w_down, layer, out_shape, out_spec, tm):
    d = gain.shape[0]
    assert w_down.shape[1] % FFN_TILE == 0
    return pl.pallas_call(
        body,
        out_shape=out_shape,
        grid=grid,
        in_specs=[*row_specs, _resident((1, d)), _resident(w_gu.shape, layer), _resident(w_down.shape, layer)],
        out_specs=out_spec,
        scratch_shapes=[pltpu.VMEM((tm, d), BF16), pltpu.VMEM((tm, d), F32)],
        compiler_params=pltpu.CompilerParams(
            dimension_semantics=("parallel",) * len(grid), vmem_limit_bytes=VMEM_LIMIT),
        name="ffn",
    )(*row_operands, gain.reshape(1, d), w_gu, w_down)


def _ffn(h, gain, w_gu, w_down, layer, mixer=None):
    t, d = h.shape
    tm = _row_tile(t)
    rows = pl.BlockSpec((tm, d), lambda i: (i, 0))
    if mixer is None:
        body, operands, specs = _ffn_body, [h], [rows]
    else:
        o, w_o, mixer_layer = mixer
        body, operands = _ffn_mixed_body, [h, o, w_o]
        specs = [rows, pl.BlockSpec((tm, o.shape[1]), lambda i: (i, 0)), _resident(w_o.shape, mixer_layer)]
    return _ffn_call(body, (t // tm,), operands, specs, gain, w_gu, w_down, layer,
                     jax.ShapeDtypeStruct((t, d), F32), rows, tm)


def _ffn_first(x, meta, gain, w_gu, w_down, layer):
    batch, s, d = x.shape
    n_meta = meta.shape[0]
    seq = n_meta + s
    tm = _row_tile(seq)
    nt = seq // tm
    assert n_meta % 8 == 0 and n_meta < tm
    window = pl.BlockSpec((pl.Element(1), pl.Element(tm), pl.Element(d)),
                          lambda b, t: (b, pl.multiple_of(jnp.maximum(t * tm - n_meta, 0), 8), 0))
    return _ffn_call(_ffn_first_body, (batch, nt), [x, meta], [window, _resident(meta.shape)],
                     gain, w_gu, w_down, layer, jax.ShapeDtypeStruct((batch * seq, d), F32),
                     pl.BlockSpec((tm, d), lambda b, t: (b * nt + t, 0)), tm)


def _ffn_last(h, gain, w_gu, w_down, layer, mixer, batch, n_meta):
    t, d = h.shape
    seq = t // batch
    s = seq - n_meta
    tm = _row_tile(s, 512)
    o, w_o, mixer_layer = mixer
    k = o.shape[1]

    def window(width):
        return pl.BlockSpec((pl.Element(1), pl.Element(tm), pl.Element(width)),
                            lambda b, i: (b, pl.multiple_of(n_meta + i * tm, 8), 0))

    return _ffn_call(_ffn_last_body, (batch, s // tm),
                     [h.reshape(batch, seq, d), o.reshape(batch, seq, k), w_o],
                     [window(d), window(k), _resident(w_o.shape, mixer_layer)],
                     gain, w_gu, w_down, layer, jax.ShapeDtypeStruct((batch, s, d), F32),
                     pl.BlockSpec((None, tm, d), lambda b, i: (b, i, 0)), tm)


def _sb_proj_body(h_ref, g_ref, w_ref, hg_ref, seg_ref, o_ref):
    n = o_ref.shape[2]
    xn = _rmsnorm_rows(h_ref[...], g_ref[...]).astype(BF16)
    seg = seg_ref[...]
    width = seg_ref.shape[0]
    for part in range(2):
        y = jnp.dot(xn, w_ref[:, part * n:(part + 1) * n].astype(BF16), preferred_element_type=F32)
        for c in range(n // width):
            cols = slice(c * width, (c + 1) * width)
            yc = y[:, cols]
            ms = jnp.dot((yc * yc).astype(BF16), seg, preferred_element_type=F32)
            o_ref[part, :, cols] = (yc * lax.rsqrt(ms + EPS) * hg_ref[part, :, cols]).astype(BF16)
    o_ref[2] = jnp.dot(xn, w_ref[:, 2 * n:].astype(BF16), preferred_element_type=F32).astype(BF16)


def _sb_proj(h, gain, w_qkv, layer, head_gains, seg):
    t, d = h.shape
    n = w_qkv.shape[2] // 3
    tm = _row_tile(t)
    return pl.pallas_call(
        _sb_proj_body,
        out_shape=jax.ShapeDtypeStruct((3, t, n), BF16),
        grid=(t // tm,),
        in_specs=[
            pl.BlockSpec((tm, d), lambda i: (i, 0)),
            _resident((1, d)),
            _resident(w_qkv.shape, layer),
            _resident(head_gains.shape),
            _resident(seg.shape),
        ],
        out_specs=pl.BlockSpec((3, tm, n), lambda i: (0, i, 0)),
        compiler_params=pltpu.CompilerParams(
            dimension_semantics=("parallel",), vmem_limit_bytes=VMEM_LIMIT),
        name="sb_proj",
    )(h, gain.reshape(1, d), w_qkv, head_gains, seg)


def _sb_attn_body(q_ref, k_ref, v_ref, cum_ref, o_ref, qp_ref, kp_ref, vp_ref, carry_ref, acc_ref,
                  *, seq):
    rows, tile = SB_ROWS, SB_TILE
    front = tile - rows
    nw = -(-seq // rows)
    lp = nw * rows
    big = jnp.float32(3.0e38)

    qp_ref[0:seq, :] = q_ref[0]
    kp_ref[front:front + seq, :] = k_ref[0]
    vp_ref[front:front + seq, :] = v_ref[0]
    for dst, lo, hi in ((qp_ref, seq, lp), (kp_ref, 0, front), (vp_ref, 0, front),
                        (kp_ref, front + seq, front + lp), (vp_ref, front + seq, front + lp)):
        if hi > lo:
            dst[lo:hi, :] = jnp.zeros((hi - lo, LANES), dst.dtype)

    lane = lax.broadcasted_iota(jnp.int32, (1, LANES), 1)
    head0 = lane < SB_HEAD_DIM
    row = lax.broadcasted_iota(jnp.int32, (2 * rows, LANES), 0)
    col = lax.broadcasted_iota(jnp.int32, (2 * rows, LANES), 1)
    causal = col < jnp.where(row >= rows, row - rows, row) + (LANES - rows)

    def mask_own(x):
        return jnp.concatenate([x[:, :tile - LANES], jnp.where(causal, x[:, tile - LANES:], 0.0)], axis=1)

    def logits(q0, kt, masked):
        q = qp_ref[pl.ds(q0, rows), :]
        zero = jnp.zeros_like(q)
        q2 = jnp.concatenate([jnp.where(head0, q, zero), jnp.where(head0, zero, q)], axis=0)
        z = lax.dot_general(q2, kp_ref[pl.ds(kt, tile), :], _NT, preferred_element_type=F32)
        sp = _softplus(z)
        if masked:
            sp = mask_own(sp)
        return z, sp.astype(BF16), vp_ref[pl.ds(kt, tile), :]

    def suffix_sums(sps):
        cr = jnp.dot(jnp.concatenate(sps, axis=0), cum_ref[...], preferred_element_type=F32)
        return [cr[u * 2 * rows:(u + 1) * 2 * rows] for u in range(len(sps))]

    def weigh(z, later, vblk, state, masked):
        if state is not None:
            carry = jnp.broadcast_to(state[0][:, 0:1], (2 * rows, LANES))
            later = later + jnp.concatenate([carry] * (tile // LANES), axis=1)
        w = jnp.exp(z - later)
        if masked:
            w = mask_own(w)
        acc = jnp.dot(w.astype(BF16), vblk, preferred_element_type=F32)
        if state is not None:
            acc = acc + state[1]
        return later[:, :LANES], acc

    def emit(i, acc):
        n_rows = min(rows, seq - i * rows)
        out = jnp.where(head0, acc[:rows], acc[rows:])
        o_ref[0, i * rows:i * rows + n_rows, :] = out[:n_rows].astype(o_ref.dtype)

    staged = [logits(i * rows, i * rows, True) for i in range(nw)]
    sums = suffix_sums([sp for _, sp, _ in staged])
    low = big
    for i in range(nw):
        z, _, vblk = staged[i]
        carry_ref[i], acc_ref[i] = sums_i, acc = weigh(z, sums[i], vblk, None, True)
        emit(i, acc)
        if i * rows >= tile:
            low = jnp.minimum(low, jnp.min(sums_i[:, 0:1]))
    carry_ref[nw] = jnp.zeros(carry_ref.shape[1:], F32)
    acc_ref[nw] = jnp.zeros(acc_ref.shape[1:], F32)

    def later_step(loop_state):
        d, _ = loop_state
        slots, nexts, staged = [], [], []
        for i in range(nw):
            valid = i * rows >= d * tile
            slots.append(jnp.where(valid, i, nw))
            nexts.append(i * rows >= (d + 1) * tile)
            kt = pl.multiple_of(jnp.maximum(i * rows - d * tile, 0), rows)
            staged.append(logits(i * rows, kt, False))
        sums = suffix_sums([sp for _, sp, _ in staged])
        low = big
        for i in range(nw):
            z, _, vblk = staged[i]
            sums_i, acc = weigh(z, sums[i], vblk, (carry_ref[slots[i]], acc_ref[slots[i]]), False)
            carry_ref[slots[i]] = sums_i
            acc_ref[slots[i]] = acc
            low = jnp.minimum(low, jnp.where(nexts[i], jnp.min(sums_i[:, 0:1]), big))
        return d + 1, low

    steps, _ = lax.while_loop(lambda loop_state: loop_state[1] < SB_SKIP_LOG, later_step,
                              (jnp.int32(1), low))

    @pl.when(steps > 1)
    def _():
        for i in range(nw):
            emit(i, acc_ref[i])


def _sb_attn(qkv, cum, batch, seq):
    n = qkv.shape[2]
    assert seq % 16 == 0 and n % LANES == 0 and SB_TILE % SB_ROWS == 0
    qkv = qkv.reshape(3, batch, seq, n)
    nw = -(-seq // SB_ROWS)
    body = functools.partial(_sb_attn_body, seq=seq)

    def spec(which):
        return pl.BlockSpec((None, 1, seq, LANES), lambda b, p, which=which: (which, b, 0, p))

    q_pad = pltpu.VMEM((nw * SB_ROWS, LANES), BF16)
    kv_pad = pltpu.VMEM((SB_TILE - SB_ROWS + nw * SB_ROWS, LANES), BF16)
    state = pltpu.VMEM((nw + 1, 2 * SB_ROWS, LANES), F32)
    out = pl.pallas_call(
        body,
        out_shape=jax.ShapeDtypeStruct((batch, seq, n), BF16),
        grid=(batch, n // LANES),
        in_specs=[spec(0), spec(1), spec(2),
                  pl.BlockSpec((SB_TILE, SB_TILE), lambda b, p: (0, 0))],
        out_specs=pl.BlockSpec((1, seq, LANES), lambda b, p: (b, 0, p)),
        scratch_shapes=[q_pad, kv_pad, kv_pad, state, state],
        compiler_params=pltpu.CompilerParams(
            dimension_semantics=("parallel", "parallel"), vmem_limit_bytes=VMEM_LIMIT),
        name="sb_attn",
    )(qkv, qkv, qkv, cum)
    return out.reshape(batch * seq, n)


def _sb_constants():
    r = jnp.arange(SEG_WIDTH)[:, None]
    c = jnp.arange(SEG_WIDTH)[None, :]
    seg = jnp.where(r // SB_HEAD_DIM == c // SB_HEAD_DIM, 1.0 / SB_HEAD_DIM, 0.0).astype(BF16)
    cum = (jnp.arange(SB_TILE)[:, None] >= jnp.arange(SB_TILE)[None, :]).astype(BF16)
    return seg, cum


def _sb_mixer(h, gain, w_qkv, g_q, g_k, layer, batch, seq):
    seg, cum = _sb_constants()
    scale = SB_HEAD_DIM ** -0.5
    head_gains = jnp.stack([jnp.tile(g_q, SB_HEADS) * scale, jnp.tile(g_k, SB_HEADS)])[:, None, :]
    qkv = _sb_proj(h, gain, w_qkv, layer, head_gains.astype(F32), seg)
    return _sb_attn(qkv, cum, batch, seq)


def _gla_proj_body(h_ref, g_ref, w_ref, wlow_ref, wup_ref, b_ref, o_ref, la_ref):
    xn = _rmsnorm_rows(h_ref[...], g_ref[...]).astype(BF16)
    g_low = jnp.dot(xn, wlow_ref[...].astype(BF16), preferred_element_type=F32).astype(BF16)
    pre = jnp.dot(g_low, wup_ref[...].astype(BF16), preferred_element_type=F32) + b_ref[...]
    la_ref[...] = (jnp.minimum(pre, 0.0) - jnp.log(1.0 + jnp.exp(-jnp.abs(pre)))) / GLA_TAU
    o_ref[...] = jnp.dot(xn, w_ref[:, :o_ref.shape[1]].astype(BF16), preferred_element_type=F32)


def _gla_proj(h, gain, w_in, w_low, w_up, layer, b_gate):
    t, d = h.shape
    dk = w_up.shape[2]
    n = w_in.shape[2] - GLA_GATE_RANK
    tm = _row_tile(t)
    return pl.pallas_call(
        _gla_proj_body,
        out_shape=(jax.ShapeDtypeStruct((t, n), F32), jax.ShapeDtypeStruct((t, dk), F32)),
        grid=(t // tm,),
        in_specs=[
            pl.BlockSpec((tm, d), lambda i: (i, 0)),
            _resident((1, d)),
            _resident(w_in.shape, layer),
            _resident(w_low.shape, layer),
            _resident(w_up.shape, layer),
            _resident((1, dk)),
        ],
        out_specs=(pl.BlockSpec((tm, n), lambda i: (i, 0)),
                   pl.BlockSpec((tm, dk), lambda i: (i, 0))),
        compiler_params=pltpu.CompilerParams(
            dimension_semantics=("parallel",), vmem_limit_bytes=VMEM_LIMIT),
        name="gla_proj",
    )(h, gain.reshape(1, d), w_in, w_low, w_up, b_gate.reshape(1, dk))


def _gla_core_body(q_ref, k_ref, v_ref, r_ref, la_ref, gout_ref, tril_ref, o_ref,
                   st_ref, oi_ref, kv_ref, qd_ref, dec_ref, *, seq, hk, group):
    c = GLA_CHUNK
    lead = seq % c
    n_full = seq // c
    first = 1 if lead else 0
    scale = hk ** -0.5
    row = lax.broadcasted_iota(jnp.int32, (c, c), 0)
    col = lax.broadcasted_iota(jnp.int32, (c, c), 1)
    causal = col <= row

    def local(chunks):
        decay = []
        for _, _, _, _, a in chunks:
            hi, lo = _split_hi_lo(a)
            decay.append(jnp.dot(tril_ref[...], jnp.concatenate([hi, lo], axis=0),
                                 preferred_element_type=F32))
        atts, kss = [], []
        for (slot, q, k, _, _), b in zip(chunks, decay):
            b_last = b[c - 1:c, :]
            qd = (q * scale * jnp.exp(b)).astype(BF16)
            kd = (k * jnp.exp(-b)).astype(BF16)
            kss.append((k * jnp.exp(b_last - b)).astype(BF16))
            atts.append(lax.dot_general(qd, kd, _NT, preferred_element_type=F32))
            qd_ref[slot] = qd
            dec_ref[slot] = jnp.broadcast_to(jnp.exp(b_last), dec_ref.shape[1:])
        for (slot, _, _, v, _), att, ks in zip(chunks, atts, kss):
            att = jnp.where(causal, att, 0.0).astype(BF16)
            oi_ref[slot] = jnp.dot(att, v.astype(BF16), preferred_element_type=F32)
            kv_ref[slot] = jnp.dot(v.T.astype(BF16), ks, preferred_element_type=F32)

    def carried(slot, r):
        st = st_ref[...]
        o = oi_ref[slot] + lax.dot_general(qd_ref[slot], st.astype(BF16), _NT, preferred_element_type=F32)
        st_ref[...] = st * dec_ref[slot][0:1, :] + kv_ref[slot]
        o = o * lax.rsqrt(jnp.mean(o * o, axis=-1, keepdims=True) + EPS) * gout_ref[...]
        return (o * (r * jax.nn.sigmoid(r))).astype(o_ref.dtype)

    def padded(ref):
        x = ref[0, 0:lead, :]
        return jnp.concatenate([jnp.zeros((c - lead, x.shape[1]), x.dtype), x], axis=0)

    def rows_of(i):
        return pl.ds(pl.multiple_of(lead + i * c, 16), c)

    def chunk_inputs(i):
        rows = rows_of(i)
        return (first + i, q_ref[0, rows, :], k_ref[0, rows, :], v_ref[0, rows, :], la_ref[0, rows, :])

    def local_group(gi, carry):
        local([chunk_inputs(gi * group + u) for u in range(group)])
        return carry

    def carried_step(i, carry):
        rows = rows_of(i)
        o_ref[0, rows, :] = carried(first + i, r_ref[0, rows, :])
        return carry

    assert n_full % group == 0
    n_groups = n_full // group
    head = [(0, padded(q_ref), padded(k_ref), padded(v_ref), padded(la_ref))] if lead else []
    local(head + [chunk_inputs(u) for u in range(group)])
    lax.fori_loop(1, n_groups, local_group, 0)

    st_ref[...] = jnp.zeros_like(st_ref)
    if lead:
        o_ref[0, 0:lead, :] = carried(0, padded(r_ref))[c - lead:, :]
    for u in range(group):
        carried_step(u, 0)
    lax.fori_loop(group, n_full, carried_step, 0, unroll=group)


def _gla_core(proj, la, g_out, batch, seq, dk, dv, *, group=16):
    hk, hv = dk // GLA_HEADS, dv // GLA_HEADS
    assert hk % LANES == 0 and hv % LANES == 0 and seq % 16 == 0 and (seq % GLA_CHUNK) % 16 == 0
    proj = proj.reshape(batch, seq, proj.shape[1])
    la = la.reshape(batch, seq, dk)
    tril = (jnp.arange(GLA_CHUNK)[:, None] >= jnp.arange(2 * GLA_CHUNK)[None, :] % GLA_CHUNK).astype(BF16)
    slots = -(-seq // GLA_CHUNK)
    group = group if (seq // GLA_CHUNK) % group == 0 else 1
    body = functools.partial(_gla_core_body, seq=seq, hk=hk, group=group)
    nk, nv = dk // hk, dv // hv
    out = pl.pallas_call(
        body,
        out_shape=jax.ShapeDtypeStruct((batch, seq, dv), BF16),
        grid=(batch, GLA_HEADS),
        in_specs=[
            pl.BlockSpec((1, seq, hk), lambda b, h: (b, 0, h)),
            pl.BlockSpec((1, seq, hk), lambda b, h: (b, 0, nk + h)),
            pl.BlockSpec((1, seq, hv), lambda b, h: (b, 0, (2 * dk) // hv + h)),
            pl.BlockSpec((1, seq, hv), lambda b, h: (b, 0, (2 * dk) // hv + nv + h)),
            pl.BlockSpec((1, seq, hk), lambda b, h: (b, 0, h)),
            pl.BlockSpec((1, hv), lambda b, h: (0, h)),
            pl.BlockSpec((GLA_CHUNK, 2 * GLA_CHUNK), lambda b, h: (0, 0)),
        ],
        out_specs=pl.BlockSpec((1, seq, hv), lambda b, h: (b, 0, h)),
        scratch_shapes=[
            pltpu.VMEM((hv, hk), F32),
            pltpu.VMEM((slots, GLA_CHUNK, hv), F32),
            pltpu.VMEM((slots, hv, hk), F32),
            pltpu.VMEM((slots, GLA_CHUNK, hk), BF16),
            pltpu.VMEM((slots, 8, hk), F32),
        ],
        compiler_params=pltpu.CompilerParams(
            dimension_semantics=("parallel", "parallel"), vmem_limit_bytes=VMEM_LIMIT),
        name="gla_core",
    )(proj, proj, proj, proj, la, g_out.reshape(1, dv), tril)
    return out.reshape(batch * seq, dv)


def _gla_mixer(h, gain, w_in, w_low, w_up, b_gate, g_out, layer, batch, seq):
    dk = w_up.shape[2]
    dv = g_out.shape[0]
    proj, la = _gla_proj(h, gain, w_in, w_low, w_up, layer, b_gate)
    return _gla_core(proj, la, g_out, batch, seq, dk, dv)


def kernel(x, meta, ffn_a_norm, ffn_a_w_gu, ffn_a_w_down, mix_norm, sb_w_qkv, sb_q_norm, sb_k_norm, sb_w_o, gla_w_in, gla_w_gate_up, gla_b_gate, gla_out_norm, gla_w_o, ffn_b_norm, ffn_b_w_gu, ffn_b_w_down):
    batch, s, d = x.shape
    depth = ffn_a_norm.shape[0]
    n_meta = meta.shape[0]
    seq = n_meta + s
    a_gu, a_down, b_gu, b_down = ffn_a_w_gu, ffn_a_w_down, ffn_b_w_gu, ffn_b_w_down
    low_pad = LANES - GLA_GATE_RANK
    w_low = jnp.pad(gla_w_in[:, :, gla_w_in.shape[2] - GLA_GATE_RANK:], ((0, 0), (0, 0), (0, low_pad)))
    w_up = jnp.pad(gla_w_gate_up, ((0, 0), (0, low_pad), (0, 0)))
    h = None
    for i in range(depth):
        if i == 0:
            h = _ffn_first(x, meta.astype(x.dtype), ffn_a_norm[i], a_gu, a_down, i)
        else:
            h = _ffn(h, ffn_a_norm[i], a_gu, a_down, i)
        j = i // 2
        if i % 2 == 0:
            o = _sb_mixer(h, mix_norm[i], sb_w_qkv, sb_q_norm[j], sb_k_norm[j], j, batch, seq)
            w_o = sb_w_o
        else:
            o = _gla_mixer(h, mix_norm[i], gla_w_in, w_low, w_up, gla_b_gate[j], gla_out_norm[j], j, batch, seq)
            w_o = gla_w_o
        if i < depth - 1:
            h = _ffn(h, ffn_b_norm[i], b_gu, b_down, i, mixer=(o, w_o, j))
        else:
            h = _ffn_last(h, ffn_b_norm[i], b_gu, b_down, i, (o, w_o, j), batch, n_meta)
    return h
```

```python
import functools

import jax
import jax.numpy as jnp
from jax import lax
from jax.experimental import pallas as pl
from jax.experimental.pallas import tpu as pltpu

F32 = jnp.float32
BF16 = jnp.bfloat16

RES_HALF = 0.5
EPS = 1e-6
SB_HEADS = 16
SB_HEAD_DIM = 64
GLA_HEADS = 4
GLA_GATE_RANK = 16
GLA_TAU = 16.0
GLA_CHUNK = 64

LANES = 128
SB_ROWS = 64
SB_TILE = 256
SEG_WIDTH = 256
SB_PAIRS = 4
SB_SKIP_LOG = 104.0
VMEM_LIMIT = 56 * 1024 * 1024

_NT = (((1,), (1,)), ((), ()))


def _row_tile(t, limit=1024):
    best = None
    for cand in range(16, min(t, limit) + 1, 16):
        if t % cand == 0:
            best = cand
    assert best is not None, t
    return best


def _split_hi_lo(a):
    hi = a.astype(BF16)
    lo = (a - hi.astype(F32)).astype(BF16)
    return hi, lo


def _rmsnorm_rows(x, g):
    ms = jnp.mean(x * x, axis=-1, keepdims=True)
    return x * lax.rsqrt(ms + EPS) * g


def _softplus(z):
    return jnp.maximum(z, 0.0) + jnp.log(1.0 + jnp.exp(-jnp.abs(z)))


FFN_TILE = 256


def _resident(shape, layer=None):
    if layer is None:
        return pl.BlockSpec(shape, lambda *_: (0,) * len(shape), pipeline_mode=pl.Buffered(1))
    return pl.BlockSpec((None, *shape[1:]), lambda *_: (layer,) + (0,) * (len(shape) - 1),
                        pipeline_mode=pl.Buffered(1))


def _ffn_rows(base_ref, g_ref, wgu_ref, wd_ref, o_ref, xn_ref, acc_ref):
    dff = wd_ref.shape[0]
    tf = FFN_TILE
    nf = dff // tf
    xn_ref[...] = _rmsnorm_rows(base_ref[...], g_ref[...]).astype(BF16)
    for j in range(nf):
        xn = xn_ref[...]
        g = jnp.dot(xn, wgu_ref[:, j * tf:(j + 1) * tf].astype(BF16), preferred_element_type=F32)
        u = jnp.dot(xn, wgu_ref[:, dff + j * tf:dff + (j + 1) * tf].astype(BF16), preferred_element_type=F32)
        a = (g * jax.nn.sigmoid(g) * u).astype(BF16)
        part = jnp.dot(a, wd_ref[j * tf:(j + 1) * tf, :].astype(BF16), preferred_element_type=F32)
        if j == 0:
            acc_ref[...] = part
        elif j < nf - 1:
            acc_ref[...] += part
        else:
            o_ref[...] = base_ref[...] + RES_HALF * (acc_ref[...] + part)


def _ffn_body(h_ref, *refs):
    _ffn_rows(h_ref, *refs)


def _ffn_mixed_body(h_ref, mo_ref, wo_ref, g_ref, wgu_ref, wd_ref, o_ref, xn_ref, acc_ref):
    o_ref[...] = h_ref[...] + jnp.dot(mo_ref[...], wo_ref[...].astype(BF16), preferred_element_type=F32)
    _ffn_rows(o_ref, g_ref, wgu_ref, wd_ref, o_ref, xn_ref, acc_ref)


def _ffn_last_body(h_ref, mo_ref, wo_ref, g_ref, wgu_ref, wd_ref, o_ref, xn_ref, acc_ref):
    o_ref[...] = h_ref[0] + jnp.dot(mo_ref[0], wo_ref[...].astype(BF16), preferred_element_type=F32)
    _ffn_rows(o_ref, g_ref, wgu_ref, wd_ref, o_ref, xn_ref, acc_ref)


def _ffn_first_body(x_ref, meta_ref, g_ref, wgu_ref, wd_ref, o_ref, xn_ref, acc_ref):
    n_meta = meta_ref.shape[0]
    xt = x_ref[0]
    lead = jnp.concatenate([meta_ref[...], xt[:xt.shape[0] - n_meta]], axis=0)
    o_ref[...] = jnp.where(pl.program_id(1) == 0, lead, xt)
    _ffn_rows(o_ref, g_ref, wgu_ref, wd_ref, o_ref, xn_ref, acc_ref)


def _ffn_call(body, grid, row_operands, row_specs, gain, w_gu, w_down, layer, out_shape, out_spec, tm):
    d = gain.shape[0]
    assert w_down.shape[1] % FFN_TILE == 0
    return pl.pallas_call(
        body,
        out_shape=out_shape,
        grid=grid,
        in_specs=[*row_specs, _resident((1, d)), _resident(w_gu.shape, layer), _resident(w_down.shape, layer)],
        out_specs=out_spec,
        scratch_shapes=[pltpu.VMEM((tm, d), BF16), pltpu.VMEM((tm, d), F32)],
        compiler_params=pltpu.CompilerParams(
            dimension_semantics=("parallel",) * len(grid), vmem_limit_bytes=VMEM_LIMIT),
        name="ffn",
    )(*row_operands, gain.reshape(1, d), w_gu, w_down)


def _ffn(h, gain, w_gu, w_down, layer, mixer=None):
    t, d = h.shape
    tm = _row_tile(t)
    rows = pl.BlockSpec((tm, d), lambda i: (i, 0))
    if mixer is None:
        body, operands, specs = _ffn_body, [h], [rows]
    else:
        o, w_o, mixer_layer = mixer
        body, operands = _ffn_mixed_body, [h, o, w_o]
        specs = [rows, pl.BlockSpec((tm, o.shape[1]), lambda i: (i, 0)), _resident(w_o.shape, mixer_layer)]
    return _ffn_call(body, (t // tm,), operands, specs, gain, w_gu, w_down, layer,
                     jax.ShapeDtypeStruct((t, d), F32), rows, tm)


def _ffn_first(x, meta, gain, w_gu, w_down, layer):
    batch, s, d = x.shape
    n_meta = meta.shape[0]
    seq = n_meta + s
    tm = _row_tile(seq)
    nt = seq // tm
    assert n_meta % 8 == 0 and n_meta < tm
    window = pl.BlockSpec((pl.Element(1), pl.Element(tm), pl.Element(d)),
                          lambda b, t: (b, pl.multiple_of(jnp.maximum(t * tm - n_meta, 0), 8), 0))
    return _ffn_call(_ffn_first_body, (batch, nt), [x, meta], [window, _resident(meta.shape)],
                     gain, w_gu, w_down, layer, jax.ShapeDtypeStruct((batch * seq, d), F32),
                     pl.BlockSpec((tm, d), lambda b, t: (b * nt + t, 0)), tm)


def _ffn_last(h, gain, w_gu, w_down, layer, mixer, batch, n_meta):
    t, d = h.shape
    seq = t // batch
    s = seq - n_meta
    tm = _row_tile(s, 512)
    o, w_o, mixer_layer = mixer
    k = o.shape[1]

    def window(width):
        return pl.BlockSpec((pl.Element(1), pl.Element(tm), pl.Element(width)),
                            lambda b, i: (b, pl.multiple_of(n_meta + i * tm, 8), 0))

    return _ffn_call(_ffn_last_body, (batch, s // tm),
                     [h.reshape(batch, seq, d), o.reshape(batch, seq, k), w_o],
                     [window(d), window(k), _resident(w_o.shape, mixer_layer)],
                     gain, w_gu, w_down, layer, jax.ShapeDtypeStruct((batch, s, d), F32),
                     pl.BlockSpec((None, tm, d), lambda b, i: (b, i, 0)), tm)


def _sb_proj_body(h_ref, g_ref, w_ref, hg_ref, seg_ref, o_ref):
    n = o_ref.shape[2]
    xn = _rmsnorm_rows(h_ref[...], g_ref[...]).astype(BF16)
    seg = seg_ref[...]
    width = seg_ref.shape[0]
    for part in range(2):
        y = jnp.dot(xn, w_ref[:, part * n:(part + 1) * n].astype(BF16), preferred_element_type=F32)
        for c in range(n // width):
            cols = slice(c * width, (c + 1) * width)
            yc = y[:, cols]
            ms = jnp.dot((yc * yc).astype(BF16), seg, preferred_element_type=F32)
            o_ref[part, :, cols] = (yc * lax.rsqrt(ms + EPS) * hg_ref[part, :, cols]).astype(BF16)
    o_ref[2] = jnp.dot(xn, w_ref[:, 2 * n:].astype(BF16), preferred_element_type=F32).astype(BF16)


def _sb_proj(h, gain, w_qkv, layer, head_gains, seg):
    t, d = h.shape
    n = w_qkv.shape[2] // 3
    tm = _row_tile(t)
    return pl.pallas_call(
        _sb_proj_body,
        out_shape=jax.ShapeDtypeStruct((3, t, n), BF16),
        grid=(t // tm,),
        in_specs=[
            pl.BlockSpec((tm, d), lambda i: (i, 0)),
            _resident((1, d)),
            _resident(w_qkv.shape, layer),
            _resident(head_gains.shape),
            _resident(seg.shape),
        ],
        out_specs=pl.BlockSpec((3, tm, n), lambda i: (0, i, 0)),
        compiler_params=pltpu.CompilerParams(
            dimension_semantics=("parallel",), vmem_limit_bytes=VMEM_LIMIT),
        name="sb_proj",
    )(h, gain.reshape(1, d), w_qkv, head_gains, seg)


def _sb_attn_pair(q_ref, k_ref, v_ref, cum_ref, o_ref, qp_ref, kp_ref, vp_ref, carry_ref, acc_ref,
                  lanes, seq):
    rows, tile = SB_ROWS, SB_TILE
    front = tile - rows
    nw = -(-seq // rows)
    lp = nw * rows
    big = jnp.float32(3.0e38)

    qp_ref[0:seq, :] = q_ref[0, :, lanes]
    kp_ref[front:front + seq, :] = k_ref[0, :, lanes]
    vp_ref[front:front + seq, :] = v_ref[0, :, lanes]
    for dst, lo, hi in ((qp_ref, seq, lp), (kp_ref, 0, front), (vp_ref, 0, front),
                        (kp_ref, front + seq, front + lp), (vp_ref, front + seq, front + lp)):
        if hi > lo:
            dst[lo:hi, :] = jnp.zeros((hi - lo, LANES), dst.dtype)

    lane = lax.broadcasted_iota(jnp.int32, (1, LANES), 1)
    head0 = lane < SB_HEAD_DIM
    row = lax.broadcasted_iota(jnp.int32, (2 * rows, LANES), 0)
    col = lax.broadcasted_iota(jnp.int32, (2 * rows, LANES), 1)
    causal = col < jnp.where(row >= rows, row - rows, row) + (LANES - rows)

    def mask_own(x):
        return jnp.concatenate([x[:, :tile - LANES], jnp.where(causal, x[:, tile - LANES:], 0.0)], axis=1)

    def logits(q0, kt, masked):
        q = qp_ref[pl.ds(q0, rows), :]
        zero = jnp.zeros_like(q)
        q2 = jnp.concatenate([jnp.where(head0, q, zero), jnp.where(head0, zero, q)], axis=0)
        z = lax.dot_general(q2, kp_ref[pl.ds(kt, tile), :], _NT, preferred_element_type=F32)
        sp = _softplus(z)
        if masked:
            sp = mask_own(sp)
        return z, sp.astype(BF16), vp_ref[pl.ds(kt, tile), :]

    def suffix_sums(sps):
        cr = jnp.dot(jnp.concatenate(sps, axis=0), cum_ref[...], preferred_element_type=F32)
        return [cr[u * 2 * rows:(u + 1) * 2 * rows] for u in range(len(sps))]

    def weigh(z, later, vblk, state, masked):
        if state is not None:
            carry = jnp.broadcast_to(state[0][:, 0:1], (2 * rows, LANES))
            later = later + jnp.concatenate([carry] * (tile // LANES), axis=1)
        w = jnp.exp(z - later)
        if masked:
            w = mask_own(w)
        acc = jnp.dot(w.astype(BF16), vblk, preferred_element_type=F32)
        if state is not None:
            acc = acc + state[1]
        return later[:, :LANES], acc

    def emit(i, acc):
        n_rows = min(rows, seq - i * rows)
        out = jnp.where(head0, acc[:rows], acc[rows:])
        o_ref[0, i * rows:i * rows + n_rows, lanes] = out[:n_rows].astype(o_ref.dtype)

    staged = [logits(i * rows, i * rows, True) for i in range(nw)]
    sums = suffix_sums([sp for _, sp, _ in staged])
    low = big
    for i in range(nw):
        z, _, vblk = staged[i]
        carry_ref[i], acc_ref[i] = sums_i, acc = weigh(z, sums[i], vblk, None, True)
        emit(i, acc)
        if i * rows >= tile:
            low = jnp.minimum(low, jnp.min(sums_i[:, 0:1]))
    carry_ref[nw] = jnp.zeros(carry_ref.shape[1:], F32)
    acc_ref[nw] = jnp.zeros(acc_ref.shape[1:], F32)

    def later_step(loop_state):
        d, _ = loop_state
        slots, nexts, staged = [], [], []
        for i in range(nw):
            valid = i * rows >= d * tile
            slots.append(jnp.where(valid, i, nw))
            nexts.append(i * rows >= (d + 1) * tile)
            kt = pl.multiple_of(jnp.maximum(i * rows - d * tile, 0), rows)
            staged.append(logits(i * rows, kt, False))
        sums = suffix_sums([sp for _, sp, _ in staged])
        low = big
        for i in range(nw):
            z, _, vblk = staged[i]
            sums_i, acc = weigh(z, sums[i], vblk, (carry_ref[slots[i]], acc_ref[slots[i]]), False)
            carry_ref[slots[i]] = sums_i
            acc_ref[slots[i]] = acc
            low = jnp.minimum(low, jnp.where(nexts[i], jnp.min(sums_i[:, 0:1]), big))
        return d + 1, low

    steps, _ = lax.while_loop(lambda loop_state: loop_state[1] < SB_SKIP_LOG, later_step,
                              (jnp.int32(1), low))

    @pl.when(steps > 1)
    def _():
        for i in range(nw):
            emit(i, acc_ref[i])


def _sb_attn_body(*refs, seq):
    for p in range(SB_PAIRS):
        _sb_attn_pair(*refs, slice(p * LANES, (p + 1) * LANES), seq)


def _sb_attn(qkv, cum, batch, seq):
    n = qkv.shape[2]
    assert seq % 16 == 0 and n % (SB_PAIRS * LANES) == 0 and SB_TILE % SB_ROWS == 0
    qkv = qkv.reshape(3, batch, seq, n)
    nw = -(-seq // SB_ROWS)
    body = functools.partial(_sb_attn_body, seq=seq)

    def spec(which):
        return pl.BlockSpec((None, 1, seq, SB_PAIRS * LANES), lambda b, p, which=which: (which, b, 0, p))

    q_pad = pltpu.VMEM((nw * SB_ROWS, LANES), BF16)
    kv_pad = pltpu.VMEM((SB_TILE - SB_ROWS + nw * SB_ROWS, LANES), BF16)
    state = pltpu.VMEM((nw + 1, 2 * SB_ROWS, LANES), F32)
    out = pl.pallas_call(
        body,
        out_shape=jax.ShapeDtypeStruct((batch, seq, n), BF16),
        grid=(batch, n // (SB_PAIRS * LANES)),
        in_specs=[spec(0), spec(1), spec(2),
                  pl.BlockSpec((SB_TILE, SB_TILE), lambda b, p: (0, 0))],
        out_specs=pl.BlockSpec((1, seq, SB_PAIRS * LANES), lambda b, p: (b, 0, p)),
        scratch_shapes=[q_pad, kv_pad, kv_pad, state, state],
        compiler_params=pltpu.CompilerParams(
            dimension_semantics=("parallel", "parallel"), vmem_limit_bytes=VMEM_LIMIT),
        name="sb_attn",
    )(qkv, qkv, qkv, cum)
    return out.reshape(batch * seq, n)


def _sb_constants():
    r = jnp.arange(SEG_WIDTH)[:, None]
    c = jnp.arange(SEG_WIDTH)[None, :]
    seg = jnp.where(r // SB_HEAD_DIM == c // SB_HEAD_DIM, 1.0 / SB_HEAD_DIM, 0.0).astype(BF16)
    cum = (jnp.arange(SB_TILE)[:, None] >= jnp.arange(SB_TILE)[None, :]).astype(BF16)
    return seg, cum


def _sb_mixer(h, gain, w_qkv, g_q, g_k, layer, batch, seq):
    seg, cum = _sb_constants()
    scale = SB_HEAD_DIM ** -0.5
    head_gains = jnp.stack([jnp.tile(g_q, SB_HEADS) * scale, jnp.tile(g_k, SB_HEADS)])[:, None, :]
    qkv = _sb_proj(h, gain, w_qkv, layer, head_gains.astype(F32), seg)
    return _sb_attn(qkv, cum, batch, seq)


def _gla_proj_body(h_ref, g_ref, w_ref, wlow_ref, wup_ref, b_ref, o_ref, la_ref):
    xn = _rmsnorm_rows(h_ref[...], g_ref[...]).astype(BF16)
    g_low = jnp.dot(xn, wlow_ref[...].astype(BF16), preferred_element_type=F32).astype(BF16)
    pre = jnp.dot(g_low, wup_ref[...].astype(BF16), preferred_element_type=F32) + b_ref[...]
    la_ref[...] = (jnp.minimum(pre, 0.0) - jnp.log(1.0 + jnp.exp(-jnp.abs(pre)))) / GLA_TAU
    o_ref[...] = jnp.dot(xn, w_ref[:, :o_ref.shape[1]].astype(BF16), preferred_element_type=F32)


def _gla_proj(h, gain, w_in, w_low, w_up, layer, b_gate):
    t, d = h.shape
    dk = w_up.shape[2]
    n = w_in.shape[2] - GLA_GATE_RANK
    tm = _row_tile(t)
    return pl.pallas_call(
        _gla_proj_body,
        out_shape=(jax.ShapeDtypeStruct((t, n), F32), jax.ShapeDtypeStruct((t, dk), F32)),
        grid=(t // tm,),
        in_specs=[
            pl.BlockSpec((tm, d), lambda i: (i, 0)),
            _resident((1, d)),
            _resident(w_in.shape, layer),
            _resident(w_low.shape, layer),
            _resident(w_up.shape, layer),
            _resident((1, dk)),
        ],
        out_specs=(pl.BlockSpec((tm, n), lambda i: (i, 0)),
                   pl.BlockSpec((tm, dk), lambda i: (i, 0))),
        compiler_params=pltpu.CompilerParams(
            dimension_semantics=("parallel",), vmem_limit_bytes=VMEM_LIMIT),
        name="gla_proj",
    )(h, gain.reshape(1, d), w_in, w_low, w_up, b_gate.reshape(1, dk))


def _gla_core_body(q_ref, k_ref, v_ref, r_ref, la_ref, gout_ref, tril_ref, o_ref,
                   st_ref, oi_ref, kv_ref, qd_ref, dec_ref, *, seq, hk, group):
    c = GLA_CHUNK
    lead = seq % c
    n_full = seq // c
    first = 1 if lead else 0
    scale = hk ** -0.5
    row = lax.broadcasted_iota(jnp.int32, (c, c), 0)
    col = lax.broadcasted_iota(jnp.int32, (c, c), 1)
    causal = col <= row

    def local(chunks):
        decay = []
        for _, _, _, _, a in chunks:
            hi, lo = _split_hi_lo(a)
            decay.append(jnp.dot(tril_ref[...], jnp.concatenate([hi, lo], axis=0),
                                 preferred_element_type=F32))
        atts, kss = [], []
        for (slot, q, k, _, _), b in zip(chunks, decay):
            b_last = b[c - 1:c, :]
            qd = (q * scale * jnp.exp(b)).astype(BF16)
            kd = (k * jnp.exp(-b)).astype(BF16)
            kss.append((k * jnp.exp(b_last - b)).astype(BF16))
            atts.append(lax.dot_general(qd, kd, _NT, preferred_element_type=F32))
            qd_ref[slot] = qd
            dec_ref[slot] = jnp.broadcast_to(jnp.exp(b_last), dec_ref.shape[1:])
        for (slot, _, _, v, _), att, ks in zip(chunks, atts, kss):
            att = jnp.where(causal, att, 0.0).astype(BF16)
            oi_ref[slot] = jnp.dot(att, v.astype(BF16), preferred_element_type=F32)
            kv_ref[slot] = jnp.dot(v.T.astype(BF16), ks, preferred_element_type=F32)

    def carried(slot, r):
        st = st_ref[...]
        o = oi_ref[slot] + lax.dot_general(qd_ref[slot], st.astype(BF16), _NT, preferred_element_type=F32)
        st_ref[...] = st * dec_ref[slot][0:1, :] + kv_ref[slot]
        o = o * lax.rsqrt(jnp.mean(o * o, axis=-1, keepdims=True) + EPS) * gout_ref[...]
        return (o * (r * jax.nn.sigmoid(r))).astype(o_ref.dtype)

    def padded(ref):
        x = ref[0, 0:lead, :]
        return jnp.concatenate([jnp.zeros((c - lead, x.shape[1]), x.dtype), x], axis=0)

    def rows_of(i):
        return pl.ds(pl.multiple_of(lead + i * c, 16), c)

    def chunk_inputs(i):
        rows = rows_of(i)
        return (first + i, q_ref[0, rows, :], k_ref[0, rows, :], v_ref[0, rows, :], la_ref[0, rows, :])

    def local_group(gi, carry):
        local([chunk_inputs(gi * group + u) for u in range(group)])
        return carry

    def carried_step(i, carry):
        rows = rows_of(i)
        o_ref[0, rows, :] = carried(first + i, r_ref[0, rows, :])
        return carry

    assert n_full % group == 0
    n_groups = n_full // group
    head = [(0, padded(q_ref), padded(k_ref), padded(v_ref), padded(la_ref))] if lead else []
    local(head + [chunk_inputs(u) for u in range(group)])
    lax.fori_loop(1, n_groups, local_group, 0)

    st_ref[...] = jnp.zeros_like(st_ref)
    if lead:
        o_ref[0, 0:lead, :] = carried(0, padded(r_ref))[c - lead:, :]
    for u in range(group):
        carried_step(u, 0)
    lax.fori_loop(group, n_full, carried_step, 0, unroll=group)


def _gla_core(proj, la, g_out, batch, seq, dk, dv, *, group=16):
    hk, hv = dk // GLA_HEADS, dv // GLA_HEADS
    assert hk % LANES == 0 and hv % LANES == 0 and seq % 16 == 0 and (seq % GLA_CHUNK) % 16 == 0
    proj = proj.reshape(batch, seq, proj.shape[1])
    la = la.reshape(batch, seq, dk)
    tril = (jnp.arange(GLA_CHUNK)[:, None] >= jnp.arange(2 * GLA_CHUNK)[None, :] % GLA_CHUNK).astype(BF16)
    slots = -(-seq // GLA_CHUNK)
    group = group if (seq // GLA_CHUNK) % group == 0 else 1
    body = functools.partial(_gla_core_body, seq=seq, hk=hk, group=group)
    nk, nv = dk // hk, dv // hv
    out = pl.pallas_call(
        body,
        out_shape=jax.ShapeDtypeStruct((batch, seq, dv), BF16),
        grid=(batch, GLA_HEADS),
        in_specs=[
            pl.BlockSpec((1, seq, hk), lambda b, h: (b, 0, h)),
            pl.BlockSpec((1, seq, hk), lambda b, h: (b, 0, nk + h)),
            pl.BlockSpec((1, seq, hv), lambda b, h: (b, 0, (2 * dk) // hv + h)),
            pl.BlockSpec((1, seq, hv), lambda b, h: (b, 0, (2 * dk) // hv + nv + h)),
            pl.BlockSpec((1, seq, hk), lambda b, h: (b, 0, h)),
            pl.BlockSpec((1, hv), lambda b, h: (0, h)),
            pl.BlockSpec((GLA_CHUNK, 2 * GLA_CHUNK), lambda b, h: (0, 0)),
        ],
        out_specs=pl.BlockSpec((1, seq, hv), lambda b, h: (b, 0, h)),
        scratch_shapes=[
            pltpu.VMEM((hv, hk), F32),
            pltpu.VMEM((slots, GLA_CHUNK, hv), F32),
            pltpu.VMEM((slots, hv, hk), F32),
            pltpu.VMEM((slots, GLA_CHUNK, hk), BF16),
            pltpu.VMEM((slots, 8, hk), F32),
        ],
        compiler_params=pltpu.CompilerParams(
            dimension_semantics=("parallel", "parallel"), vmem_limit_bytes=VMEM_LIMIT),
        name="gla_core",
    )(proj, proj, proj, proj, la, g_out.reshape(1, dv), tril)
    return out.reshape(batch * seq, dv)


def _gla_mixer(h, gain, w_in, w_low, w_up, b_gate, g_out, layer, batch, seq):
    dk = w_up.shape[2]
    dv = g_out.shape[0]
    proj, la = _gla_proj(h, gain, w_in, w_low, w_up, layer, b_gate)
    return _gla_core(proj, la, g_out, batch, seq, dk, dv)


def kernel(x, meta, ffn_a_norm, ffn_a_w_gu, ffn_a_w_down, mix_norm, sb_w_qkv, sb_q_norm, sb_k_norm, sb_w_o, gla_w_in, gla_w_gate_up, gla_b_gate, gla_out_norm, gla_w_o, ffn_b_norm, ffn_b_w_gu, ffn_b_w_down):
    batch, s, d = x.shape
    depth = ffn_a_norm.shape[0]
    n_meta = meta.shape[0]
    seq = n_meta + s
    a_gu, a_down, b_gu, b_down = ffn_a_w_gu, ffn_a_w_down, ffn_b_w_gu, ffn_b_w_down
    low_pad = LANES - GLA_GATE_RANK
    w_low = jnp.pad(gla_w_in[:, :, gla_w_in.shape[2] - GLA_GATE_RANK:], ((0, 0), (0, 0), (0, low_pad)))
    w_up = jnp.pad(gla_w_gate_up, ((0, 0), (0, low_pad), (0, 0)))
    h = None
    for i in range(depth):
        if i == 0:
            h = _ffn_first(x, meta.astype(x.dtype), ffn_a_norm[i], a_gu, a_down, i)
        else:
            h = _ffn(h, ffn_a_norm[i], a_gu, a_down, i)
        j = i // 2
        if i % 2 == 0:
            o = _sb_mixer(h, mix_norm[i], sb_w_qkv, sb_q_norm[j], sb_k_norm[j], j, batch, seq)
            w_o = sb_w_o
        else:
            o = _gla_mixer(h, mix_norm[i], gla_w_in, w_low, w_up, gla_b_gate[j], gla_out_norm[j], j, batch, seq)
            w_o = gla_w_o
        if i < depth - 1:
            h = _ffn(h, ffn_b_norm[i], b_gu, b_down, i, mixer=(o, w_o, j))
        else:
            h = _ffn_last(h, ffn_b_norm[i], b_gu, b_down, i, (o, w_o, j), batch, n_meta)
    return h
```

```python
import functools

import jax
import jax.numpy as jnp
from jax import lax
from jax.experimental import pallas as pl
from jax.experimental.pallas import tpu as pltpu

F32 = jnp.float32
BF16 = jnp.bfloat16

N_META = 16
RES_HALF = 0.5
EPS = 1e-6
SB_HEADS = 16
SB_HEAD_DIM = 64
GLA_HEADS = 4
GLA_GATE_RANK = 16
GLA_TAU = 16.0
GLA_CHUNK = 64

LANES = 128
SB_ROWS = 64
SB_TILE = 256
SEG_WIDTH = 256
SB_SKIP_LOG = 104.0
VMEM_LIMIT = 56 * 1024 * 1024

_NT = (((1,), (1,)), ((), ()))


def _row_tile(t, limit=1024):
    best = None
    for cand in range(16, min(t, limit) + 1, 16):
        if t % cand == 0:
            best = cand
    assert best is not None, t
    return best


def _split_hi_lo(a):
    hi = a.astype(BF16)
    lo = (a - hi.astype(F32)).astype(BF16)
    return hi, lo


def _rmsnorm_rows(x, g):
    ms = jnp.mean(x * x, axis=-1, keepdims=True)
    return x * lax.rsqrt(ms + EPS) * g


def _softplus(z):
    return jnp.maximum(z, 0.0) + jnp.log(1.0 + jnp.exp(-jnp.abs(z)))


FFN_TILE = 256


def _resident(shape, layer=None):
    if layer is None:
        return pl.BlockSpec(shape, lambda *_: (0,) * len(shape), pipeline_mode=pl.Buffered(1))
    return pl.BlockSpec((None, *shape[1:]), lambda *_: (layer,) + (0,) * (len(shape) - 1),
                        pipeline_mode=pl.Buffered(1))


def _ffn_rows(base_ref, g_ref, wgu_ref, wd_ref, o_ref, xn_ref, acc_ref):
    dff = wd_ref.shape[0]
    tf = FFN_TILE
    nf = dff // tf
    xn_ref[...] = _rmsnorm_rows(base_ref[...], g_ref[...]).astype(BF16)
    for j in range(nf):
        xn = xn_ref[...]
        g = jnp.dot(xn, wgu_ref[:, j * tf:(j + 1) * tf].astype(BF16), preferred_element_type=F32)
        u = jnp.dot(xn, wgu_ref[:, dff + j * tf:dff + (j + 1) * tf].astype(BF16), preferred_element_type=F32)
        a = (g * jax.nn.sigmoid(g) * u).astype(BF16)
        part = jnp.dot(a, wd_ref[j * tf:(j + 1) * tf, :].astype(BF16), preferred_element_type=F32)
        if j == 0:
            acc_ref[...] = part
        elif j < nf - 1:
            acc_ref[...] += part
        else:
            o_ref[...] = base_ref[...] + RES_HALF * (acc_ref[...] + part)


def _ffn_body(h_ref, *refs):
    _ffn_rows(h_ref, *refs)


def _ffn_mixed_body(h_ref, mo_ref, wo_ref, g_ref, wgu_ref, wd_ref, o_ref, xn_ref, acc_ref):
    o_ref[...] = h_ref[...] + jnp.dot(mo_ref[...], wo_ref[...].astype(BF16), preferred_element_type=F32)
    _ffn_rows(o_ref, g_ref, wgu_ref, wd_ref, o_ref, xn_ref, acc_ref)


def _ffn_last_body(h_ref, mo_ref, wo_ref, g_ref, wgu_ref, wd_ref, o_ref, xn_ref, acc_ref):
    o_ref[...] = h_ref[0] + jnp.dot(mo_ref[0], wo_ref[...].astype(BF16), preferred_element_type=F32)
    _ffn_rows(o_ref, g_ref, wgu_ref, wd_ref, o_ref, xn_ref, acc_ref)


def _ffn_first_body(x_ref, meta_ref, g_ref, wgu_ref, wd_ref, o_ref, xn_ref, acc_ref):
    n_meta = meta_ref.shape[0]
    xt = x_ref[0]
    lead = jnp.concatenate([meta_ref[...], xt[:xt.shape[0] - n_meta]], axis=0)
    o_ref[...] = jnp.where(pl.program_id(1) == 0, lead, xt)
    _ffn_rows(o_ref, g_ref, wgu_ref, wd_ref, o_ref, xn_ref, acc_ref)


def _ffn_call(body, grid, row_operands, row_specs, gain, w_gu, w_down, layer, out_shape, out_spec, tm):
    d = gain.shape[0]
    assert w_down.shape[1] % FFN_TILE == 0
    return pl.pallas_call(
        body,
        out_shape=out_shape,
        grid=grid,
        in_specs=[*row_specs, _resident((1, d)), _resident(w_gu.shape, layer), _resident(w_down.shape, layer)],
        out_specs=out_spec,
        scratch_shapes=[pltpu.VMEM((tm, d), BF16), pltpu.VMEM((tm, d), F32)],
        compiler_params=pltpu.CompilerParams(
            dimension_semantics=("parallel",) * len(grid), vmem_limit_bytes=VMEM_LIMIT),
        name="ffn",
    )(*row_operands, gain.reshape(1, d), w_gu, w_down)


def _ffn(h, gain, w_gu, w_down, layer, mixer=None):
    t, d = h.shape
    tm = _row_tile(t)
    rows = pl.BlockSpec((tm, d), lambda i: (i, 0))
    if mixer is None:
        body, operands, specs = _ffn_body, [h], [rows]
    else:
        o, w_o, mixer_layer = mixer
        body, operands = _ffn_mixed_body, [h, o, w_o]
        specs = [rows, pl.BlockSpec((tm, o.shape[1]), lambda i: (i, 0)), _resident(w_o.shape, mixer_layer)]
    return _ffn_call(body, (t // tm,), operands, specs, gain, w_gu, w_down, layer,
                     jax.ShapeDtypeStruct((t, d), F32), rows, tm)


def _ffn_first(x, meta, gain, w_gu, w_down, layer):
    batch, s, d = x.shape
    n_meta = meta.shape[0]
    seq = n_meta + s
    tm = _row_tile(seq)
    nt = seq // tm
    assert n_meta % 8 == 0 and n_meta < tm
    window = pl.BlockSpec((pl.Element(1), pl.Element(tm), pl.Element(d)),
                          lambda b, t: (b, pl.multiple_of(jnp.maximum(t * tm - n_meta, 0), 8), 0))
    return _ffn_call(_ffn_first_body, (batch, nt), [x, meta], [window, _resident(meta.shape)],
                     gain, w_gu, w_down, layer, jax.ShapeDtypeStruct((batch * seq, d), F32),
                     pl.BlockSpec((tm, d), lambda b, t: (b * nt + t, 0)), tm)


def _ffn_last(h, gain, w_gu, w_down, layer, mixer, batch, n_meta):
    t, d = h.shape
    seq = t // batch
    s = seq - n_meta
    tm = _row_tile(s, 512)
    o, w_o, mixer_layer = mixer
    k = o.shape[1]

    def window(width):
        return pl.BlockSpec((pl.Element(1), pl.Element(tm), pl.Element(width)),
                            lambda b, i: (b, pl.multiple_of(n_meta + i * tm, 8), 0))

    return _ffn_call(_ffn_last_body, (batch, s // tm),
                     [h.reshape(batch, seq, d), o.reshape(batch, seq, k), w_o],
                     [window(d), window(k), _resident(w_o.shape, mixer_layer)],
                     gain, w_gu, w_down, layer, jax.ShapeDtypeStruct((batch, s, d), F32),
                     pl.BlockSpec((None, tm, d), lambda b, i: (b, i, 0)), tm)


def _sb_proj_body(h_ref, g_ref, w_ref, hg_ref, seg_ref, o_ref):
    n = o_ref.shape[2]
    xn = _rmsnorm_rows(h_ref[...], g_ref[...]).astype(BF16)
    seg = seg_ref[...]
    width = seg_ref.shape[0]
    for part in range(2):
        y = jnp.dot(xn, w_ref[:, part * n:(part + 1) * n].astype(BF16), preferred_element_type=F32)
        for c in range(n // width):
            cols = slice(c * width, (c + 1) * width)
            yc = y[:, cols]
            ms = jnp.dot((yc * yc).astype(BF16), seg, preferred_element_type=F32)
            o_ref[part, :, cols] = (yc * lax.rsqrt(ms + EPS) * hg_ref[part, :, cols]).astype(BF16)
    o_ref[2] = jnp.dot(xn, w_ref[:, 2 * n:].astype(BF16), preferred_element_type=F32).astype(BF16)


def _sb_proj(h, gain, w_qkv, layer, head_gains, seg):
    t, d = h.shape
    n = w_qkv.shape[2] // 3
    tm = _row_tile(t)
    return pl.pallas_call(
        _sb_proj_body,
        out_shape=jax.ShapeDtypeStruct((3, t, n), BF16),
        grid=(t // tm,),
        in_specs=[
            pl.BlockSpec((tm, d), lambda i: (i, 0)),
            _resident((1, d)),
            _resident(w_qkv.shape, layer),
            _resident(head_gains.shape),
            _resident(seg.shape),
        ],
        out_specs=pl.BlockSpec((3, tm, n), lambda i: (0, i, 0)),
        compiler_params=pltpu.CompilerParams(
            dimension_semantics=("parallel",), vmem_limit_bytes=VMEM_LIMIT),
        name="sb_proj",
    )(h, gain.reshape(1, d), w_qkv, head_gains, seg)


def _sb_attn_body(q_ref, k_ref, v_ref, cum_ref, o_ref, qp_ref, kp_ref, vp_ref, carry_ref, acc_ref,
                  *, seq):
    rows, tile = SB_ROWS, SB_TILE
    front = tile - rows
    nw = -(-seq // rows)
    lp = nw * rows
    big = jnp.float32(3.0e38)

    qp_ref[0:seq, :] = q_ref[0]
    kp_ref[front:front + seq, :] = k_ref[0]
    vp_ref[front:front + seq, :] = v_ref[0]
    for dst, lo, hi in ((qp_ref, seq, lp), (kp_ref, 0, front), (vp_ref, 0, front),
                        (kp_ref, front + seq, front + lp), (vp_ref, front + seq, front + lp)):
        if hi > lo:
            dst[lo:hi, :] = jnp.zeros((hi - lo, LANES), dst.dtype)

    lane = lax.broadcasted_iota(jnp.int32, (1, LANES), 1)
    head0 = lane < SB_HEAD_DIM
    row = lax.broadcasted_iota(jnp.int32, (2 * rows, LANES), 0)
    col = lax.broadcasted_iota(jnp.int32, (2 * rows, LANES), 1)
    causal = col < jnp.where(row >= rows, row - rows, row) + (LANES - rows)

    def mask_own(x):
        return jnp.concatenate([x[:, :tile - LANES], jnp.where(causal, x[:, tile - LANES:], 0.0)], axis=1)

    def logits(q0, kt, masked):
        q = qp_ref[pl.ds(q0, rows), :]
        zero = jnp.zeros_like(q)
        q2 = jnp.concatenate([jnp.where(head0, q, zero), jnp.where(head0, zero, q)], axis=0)
        z = lax.dot_general(q2, kp_ref[pl.ds(kt, tile), :], _NT, preferred_element_type=F32)
        sp = _softplus(z)
        if masked:
            sp = mask_own(sp)
        return z, sp.astype(BF16), vp_ref[pl.ds(kt, tile), :]

    def suffix_sums(sps):
        cr = jnp.dot(jnp.concatenate(sps, axis=0), cum_ref[...], preferred_element_type=F32)
        return [cr[u * 2 * rows:(u + 1) * 2 * rows] for u in range(len(sps))]

    def weigh(z, later, vblk, state, masked):
        if state is not None:
            carry = jnp.broadcast_to(state[0][:, 0:1], (2 * rows, LANES))
            later = later + jnp.concatenate([carry] * (tile // LANES), axis=1)
        w = jnp.exp(z - later)
        if masked:
            w = mask_own(w)
        acc = jnp.dot(w.astype(BF16), vblk, preferred_element_type=F32)
        if state is not None:
            acc = acc + state[1]
        return later[:, :LANES], acc

    def emit(i, acc):
        n_rows = min(rows, seq - i * rows)
        out = jnp.where(head0, acc[:rows], acc[rows:])
        o_ref[0, i * rows:i * rows + n_rows, :] = out[:n_rows].astype(o_ref.dtype)

    staged = [logits(i * rows, i * rows, True) for i in range(nw)]
    sums = suffix_sums([sp for _, sp, _ in staged])
    low = big
    for i in range(nw):
        z, _, vblk = staged[i]
        carry_ref[i], acc_ref[i] = sums_i, acc = weigh(z, sums[i], vblk, None, True)
        emit(i, acc)
        if i * rows >= tile:
            low = jnp.minimum(low, jnp.min(sums_i[:, 0:1]))
    carry_ref[nw] = jnp.zeros(carry_ref.shape[1:], F32)
    acc_ref[nw] = jnp.zeros(acc_ref.shape[1:], F32)

    def later_step(loop_state):
        d, _ = loop_state
        slots, nexts, staged = [], [], []
        for i in range(nw):
            valid = i * rows >= d * tile
            slots.append(jnp.where(valid, i, nw))
            nexts.append(i * rows >= (d + 1) * tile)
            kt = pl.multiple_of(jnp.maximum(i * rows - d * tile, 0), rows)
            staged.append(logits(i * rows, kt, False))
        sums = suffix_sums([sp for _, sp, _ in staged])
        low = big
        for i in range(nw):
            z, _, vblk = staged[i]
            sums_i, acc = weigh(z, sums[i], vblk, (carry_ref[slots[i]], acc_ref[slots[i]]), False)
            carry_ref[slots[i]] = sums_i
            acc_ref[slots[i]] = acc
            low = jnp.minimum(low, jnp.where(nexts[i], jnp.min(sums_i[:, 0:1]), big))
        return d + 1, low

    steps, _ = lax.while_loop(lambda loop_state: loop_state[1] < SB_SKIP_LOG, later_step,
                              (jnp.int32(1), low))

    @pl.when(steps > 1)
    def _():
        for i in range(nw):
            emit(i, acc_ref[i])


def _sb_attn(qkv, cum, batch, seq):
    n = qkv.shape[2]
    assert seq % 16 == 0 and n % LANES == 0 and SB_TILE % SB_ROWS == 0
    qkv = qkv.reshape(3, batch, seq, n)
    nw = -(-seq // SB_ROWS)
    body = functools.partial(_sb_attn_body, seq=seq)

    def spec(which):
        return pl.BlockSpec((None, 1, seq, LANES), lambda b, p, which=which: (which, b, 0, p))

    q_pad = pltpu.VMEM((nw * SB_ROWS, LANES), BF16)
    kv_pad = pltpu.VMEM((SB_TILE - SB_ROWS + nw * SB_ROWS, LANES), BF16)
    state = pltpu.VMEM((nw + 1, 2 * SB_ROWS, LANES), F32)
    out = pl.pallas_call(
        body,
        out_shape=jax.ShapeDtypeStruct((batch, seq, n), BF16),
        grid=(batch, n // LANES),
        in_specs=[spec(0), spec(1), spec(2),
                  pl.BlockSpec((SB_TILE, SB_TILE), lambda b, p: (0, 0))],
        out_specs=pl.BlockSpec((1, seq, LANES), lambda b, p: (b, 0, p)),
        scratch_shapes=[q_pad, kv_pad, kv_pad, state, state],
        compiler_params=pltpu.CompilerParams(
            dimension_semantics=("parallel", "parallel"), vmem_limit_bytes=VMEM_LIMIT),
        name="sb_attn",
    )(qkv, qkv, qkv, cum)
    return out.reshape(batch * seq, n)


def _sb_constants():
    r = jnp.arange(SEG_WIDTH)[:, None]
    c = jnp.arange(SEG_WIDTH)[None, :]
    seg = jnp.where(r // SB_HEAD_DIM == c // SB_HEAD_DIM, 1.0 / SB_HEAD_DIM, 0.0).astype(BF16)
    cum = (jnp.arange(SB_TILE)[:, None] >= jnp.arange(SB_TILE)[None, :]).astype(BF16)
    return seg, cum


def _sb_mixer(h, gain, w_qkv, g_q, g_k, layer, batch, seq):
    seg, cum = _sb_constants()
    scale = SB_HEAD_DIM ** -0.5
    head_gains = jnp.stack([jnp.tile(g_q, SB_HEADS) * scale, jnp.tile(g_k, SB_HEADS)])[:, None, :]
    qkv = _sb_proj(h, gain, w_qkv, layer, head_gains.astype(F32), seg)
    return _sb_attn(qkv, cum, batch, seq)


def _gla_proj_body(h_ref, g_ref, w_ref, wlow_ref, wup_ref, b_ref, o_ref, v_ref, la_ref):
    xn = _rmsnorm_rows(h_ref[...], g_ref[...]).astype(BF16)
    g_low = jnp.dot(xn, wlow_ref[...].astype(BF16), preferred_element_type=F32).astype(BF16)
    pre = jnp.dot(g_low, wup_ref[...].astype(BF16), preferred_element_type=F32) + b_ref[...]
    la_ref[...] = (jnp.minimum(pre, 0.0) - jnp.log(1.0 + jnp.exp(-jnp.abs(pre)))) / GLA_TAU
    dk2, dv = 2 * la_ref.shape[1], v_ref.shape[1]
    o_ref[:, :dk2] = jnp.dot(xn, w_ref[:, :dk2].astype(BF16), preferred_element_type=F32)
    v_ref[...] = jnp.dot(xn, w_ref[:, dk2:dk2 + dv].astype(BF16), preferred_element_type=F32).astype(BF16)
    o_ref[:, dk2:] = jnp.dot(xn, w_ref[:, dk2 + dv:dk2 + 2 * dv].astype(BF16), preferred_element_type=F32)


def _gla_proj(h, gain, w_in, w_low, w_up, layer, b_gate):
    t, d = h.shape
    dk = w_up.shape[2]
    dv = (w_in.shape[2] - GLA_GATE_RANK - 2 * dk) // 2
    n = 2 * dk + dv
    tm = _row_tile(t)
    return pl.pallas_call(
        _gla_proj_body,
        out_shape=(jax.ShapeDtypeStruct((t, n), F32), jax.ShapeDtypeStruct((t, dv), BF16),
                   jax.ShapeDtypeStruct((t, dk), F32)),
        grid=(t // tm,),
        in_specs=[
            pl.BlockSpec((tm, d), lambda i: (i, 0)),
            _resident((1, d)),
            _resident(w_in.shape, layer),
            _resident(w_low.shape, layer),
            _resident(w_up.shape, layer),
            _resident((1, dk)),
        ],
        out_specs=(pl.BlockSpec((tm, n), lambda i: (i, 0)),
                   pl.BlockSpec((tm, dv), lambda i: (i, 0)),
                   pl.BlockSpec((tm, dk), lambda i: (i, 0))),
        compiler_params=pltpu.CompilerParams(
            dimension_semantics=("parallel",), vmem_limit_bytes=VMEM_LIMIT),
        name="gla_proj",
    )(h, gain.reshape(1, d), w_in, w_low, w_up, b_gate.reshape(1, dk))


def _gla_core_body(q_ref, k_ref, v_ref, r_ref, la_ref, gout_ref, tril_ref, o_ref,
                   st_ref, oi_ref, kv_ref, qd_ref, dec_ref, *, seq, hk, group):
    c = GLA_CHUNK
    lead = seq % c
    n_full = seq // c
    first = 1 if lead else 0
    scale = hk ** -0.5
    row = lax.broadcasted_iota(jnp.int32, (c, c), 0)
    col = lax.broadcasted_iota(jnp.int32, (c, c), 1)
    causal = col <= row

    def local(chunks):
        decay = []
        for _, _, _, _, a in chunks:
            hi, lo = _split_hi_lo(a)
            decay.append(jnp.dot(tril_ref[...], jnp.concatenate([hi, lo], axis=0),
                                 preferred_element_type=F32))
        atts, kss = [], []
        for (slot, q, k, _, _), b in zip(chunks, decay):
            b_last = b[c - 1:c, :]
            qd = (q * scale * jnp.exp(b)).astype(BF16)
            kd = (k * jnp.exp(-b)).astype(BF16)
            kss.append((k * jnp.exp(b_last - b)).astype(BF16))
            atts.append(lax.dot_general(qd, kd, _NT, preferred_element_type=F32))
            qd_ref[slot] = qd
            dec_ref[slot] = jnp.broadcast_to(jnp.exp(b_last), dec_ref.shape[1:])
        for (slot, _, _, v, _), att, ks in zip(chunks, atts, kss):
            att = jnp.where(causal, att, 0.0).astype(BF16)
            oi_ref[slot] = jnp.dot(att, v.astype(BF16), preferred_element_type=F32)
            kv_ref[slot] = jnp.dot(v.astype(F32).T.astype(BF16), ks, preferred_element_type=F32)

    def carried(slot, r):
        st = st_ref[...]
        o = oi_ref[slot] + lax.dot_general(qd_ref[slot], st.astype(BF16), _NT, preferred_element_type=F32)
        st_ref[...] = st * dec_ref[slot][0:1, :] + kv_ref[slot]
        o = o * lax.rsqrt(jnp.mean(o * o, axis=-1, keepdims=True) + EPS) * gout_ref[...]
        return (o * (r * jax.nn.sigmoid(r))).astype(o_ref.dtype)

    def padded(ref):
        x = ref[0, 0:lead, :]
        return jnp.concatenate([jnp.zeros((c - lead, x.shape[1]), x.dtype), x], axis=0)

    def rows_of(i):
        return pl.ds(pl.multiple_of(lead + i * c, 16), c)

    def chunk_inputs(i):
        rows = rows_of(i)
        return (first + i, q_ref[0, rows, :], k_ref[0, rows, :], v_ref[0, rows, :], la_ref[0, rows, :])

    def local_group(gi, carry):
        local([chunk_inputs(gi * group + u) for u in range(group)])
        return carry

    def carried_step(i, carry):
        rows = rows_of(i)
        o_ref[0, rows, :] = carried(first + i, r_ref[0, rows, :])
        return carry

    assert n_full % group == 0
    n_groups = n_full // group
    head = [(0, padded(q_ref), padded(k_ref), padded(v_ref), padded(la_ref))] if lead else []
    local(head + [chunk_inputs(u) for u in range(group)])
    lax.fori_loop(1, n_groups, local_group, 0)

    st_ref[...] = jnp.zeros_like(st_ref)
    if lead:
        o_ref[0, 0:lead, :] = carried(0, padded(r_ref))[c - lead:, :]
    for u in range(group):
        carried_step(u, 0)
    lax.fori_loop(group, n_full, carried_step, 0, unroll=group)


def _gla_core(proj, v, la, g_out, batch, seq, dk, dv, *, group=16):
    hk, hv = dk // GLA_HEADS, dv // GLA_HEADS
    assert hk % LANES == 0 and hv % LANES == 0 and seq % 16 == 0 and (seq % GLA_CHUNK) % 16 == 0
    proj = proj.reshape(batch, seq, proj.shape[1])
    v = v.reshape(batch, seq, dv)
    la = la.reshape(batch, seq, dk)
    tril = (jnp.arange(GLA_CHUNK)[:, None] >= jnp.arange(2 * GLA_CHUNK)[None, :] % GLA_CHUNK).astype(BF16)
    slots = -(-seq // GLA_CHUNK)
    group = group if (seq // GLA_CHUNK) % group == 0 else 1
    body = functools.partial(_gla_core_body, seq=seq, hk=hk, group=group)
    nk = dk // hk
    out = pl.pallas_call(
        body,
        out_shape=jax.ShapeDtypeStruct((batch, seq, dv), BF16),
        grid=(batch, GLA_HEADS),
        in_specs=[
            pl.BlockSpec((1, seq, hk), lambda b, h: (b, 0, h)),
            pl.BlockSpec((1, seq, hk), lambda b, h: (b, 0, nk + h)),
            pl.BlockSpec((1, seq, hv), lambda b, h: (b, 0, h)),
            pl.BlockSpec((1, seq, hv), lambda b, h: (b, 0, (2 * dk) // hv + h)),
            pl.BlockSpec((1, seq, hk), lambda b, h: (b, 0, h)),
            pl.BlockSpec((1, hv), lambda b, h: (0, h)),
            pl.BlockSpec((GLA_CHUNK, 2 * GLA_CHUNK), lambda b, h: (0, 0)),
        ],
        out_specs=pl.BlockSpec((1, seq, hv), lambda b, h: (b, 0, h)),
        scratch_shapes=[
            pltpu.VMEM((hv, hk), F32),
            pltpu.VMEM((slots, GLA_CHUNK, hv), F32),
            pltpu.VMEM((slots, hv, hk), F32),
            pltpu.VMEM((slots, GLA_CHUNK, hk), BF16),
            pltpu.VMEM((slots, 8, hk), F32),
        ],
        compiler_params=pltpu.CompilerParams(
            dimension_semantics=("parallel", "parallel"), vmem_limit_bytes=VMEM_LIMIT),
        name="gla_core",
    )(proj, proj, v, proj, la, g_out.reshape(1, dv), tril)
    return out.reshape(batch * seq, dv)


def _gla_mixer(h, gain, w_in, w_low, w_up, b_gate, g_out, layer, batch, seq):
    dk = w_up.shape[2]
    dv = g_out.shape[0]
    proj, v, la = _gla_proj(h, gain, w_in, w_low, w_up, layer, b_gate)
    return _gla_core(proj, v, la, g_out, batch, seq, dk, dv)


def kernel(x, meta, ffn_a_norm, ffn_a_w_gu, ffn_a_w_down, mix_norm, sb_w_qkv, sb_q_norm, sb_k_norm, sb_w_o, gla_w_in, gla_w_gate_up, gla_b_gate, gla_out_norm, gla_w_o, ffn_b_norm, ffn_b_w_gu, ffn_b_w_down):
    batch, s, d = x.shape
    depth = ffn_a_norm.shape[0]
    n_meta = meta.shape[0]
    seq = n_meta + s
    a_gu, a_down, b_gu, b_down = ffn_a_w_gu, ffn_a_w_down, ffn_b_w_gu, ffn_b_w_down
    low_pad = LANES - GLA_GATE_RANK
    w_low = jnp.pad(gla_w_in[:, :, gla_w_in.shape[2] - GLA_GATE_RANK:], ((0, 0), (0, 0), (0, low_pad)))
    w_up = jnp.pad(gla_w_gate_up, ((0, 0), (0, low_pad), (0, 0)))
    h = None
    for i in range(depth):
        if i == 0:
            h = _ffn_first(x, meta.astype(x.dtype), ffn_a_norm[i], a_gu, a_down, i)
        else:
            h = _ffn(h, ffn_a_norm[i], a_gu, a_down, i)
        j = i // 2
        if i % 2 == 0:
            o = _sb_mixer(h, mix_norm[i], sb_w_qkv, sb_q_norm[j], sb_k_norm[j], j, batch, seq)
            w_o = sb_w_o
        else:
            o = _gla_mixer(h, mix_norm[i], gla_w_in, w_low, w_up, gla_b_gate[j], gla_out_norm[j], j, batch, seq)
            w_o = gla_w_o
        if i < depth - 1:
            h = _ffn(h, ffn_b_norm[i], b_gu, b_down, i, mixer=(o, w_o, j))
        else:
            h = _ffn_last(h, ffn_b_norm[i], b_gu, b_down, i, (o, w_o, j), batch, n_meta)
    return h
```
